```python
import math
import jax, jax.numpy as jnp
from jax import lax
import numpy as np

D_MODEL = 1024
BATCH = 4
SEQ = 8192
DEPTH = 4

D_FF = 2816
EPS = 1e-6
SSD_D_INNER = D_MODEL
SSD_HEAD_DIM = 64
SSD_HEADS = SSD_D_INNER // SSD_HEAD_DIM
SSD_GROUPS = 2
SSD_STATE = 128
SSD_CONV = 4
SSD_CHUNK = 128
SSD_XBC = SSD_D_INNER + 2 * SSD_GROUPS * SSD_STATE
DA_HEAD_DIM = 64
DA_HEADS = D_MODEL // (2 * DA_HEAD_DIM)
DA_QK = DA_HEADS * 2 * DA_HEAD_DIM
DA_V = DA_HEADS * 2 * DA_HEAD_DIM
DA_EPS = 1e-5
Q_BLOCK = 128
CONV_CH = D_MODEL
CONV_WIDTH = 31
N_BRANCH = 3
IN_SIZES = (N_BRANCH * D_MODEL, SSD_D_INNER, SSD_XBC, SSD_HEADS, DA_QK, DA_QK, DA_V, 2 * CONV_CH)
IN_COLS = sum(IN_SIZES)

kernel_name = "hybrid_ssd_diffattn_conformer_macaron"


def _split_points(sizes):
    pts, acc = [], 0
    for s in sizes[:-1]:
        acc += s
        pts.append(acc)
    return pts


def rms_norm(x, g, eps=EPS):
    xf = x.astype(jnp.float32)
    y = xf * lax.rsqrt(jnp.mean(xf * xf, axis=-1, keepdims=True) + eps)
    return (y * g.astype(jnp.float32)).astype(x.dtype)


def layer_norm(x, g, b, eps=1e-5):
    xf = x.astype(jnp.float32)
    mu = jnp.mean(xf, axis=-1, keepdims=True)
    var = jnp.mean(jnp.square(xf - mu), axis=-1, keepdims=True)
    y = (xf - mu) * lax.rsqrt(var + eps)
    return (y * g.astype(jnp.float32) + b.astype(jnp.float32)).astype(x.dtype)


def gated_group_rms_norm(y, z, g):
    b, s, d = y.shape
    u = (y * jax.nn.silu(z)).astype(jnp.float32).reshape(b, s, SSD_GROUPS, d // SSD_GROUPS)
    u = u * lax.rsqrt(jnp.mean(u * u, axis=-1, keepdims=True) + EPS)
    return (u.reshape(b, s, d) * g.astype(jnp.float32)).astype(y.dtype)


def swiglu(h, w13, w2):
    gate, up = jnp.split(h @ w13, 2, axis=-1)
    return (jax.nn.silu(gate) * up) @ w2


def causal_dwconv(x, w, b):
    k = w.shape[0]
    y = lax.conv_general_dilated(
        x, w[:, None, :].astype(x.dtype), window_strides=(1,), padding=[(k - 1, 0)],
        dimension_numbers=("NWC", "WIO", "NWC"), feature_group_count=x.shape[-1])
    return y + b


def ssd_scan(x, dt, a_head, bm, cm):
    b, s, h, p = x.shape
    g, n = bm.shape[-2:]
    hg = h // g
    nc = s // SSD_CHUNK
    f32 = jnp.float32
    xdt = (x.astype(f32) * dt[..., None]).reshape(b, nc, SSD_CHUNK, g, hg, p)
    a = (dt * a_head.astype(f32)).reshape(b, nc, SSD_CHUNK, g, hg).transpose(0, 3, 4, 1, 2)
    bc = bm.astype(f32).reshape(b, nc, SSD_CHUNK, g, n)
    cc = cm.astype(f32).reshape(b, nc, SSD_CHUNK, g, n)
    a_cum = jnp.cumsum(a, axis=-1)
    causal = jnp.tril(jnp.ones((SSD_CHUNK, SSD_CHUNK), dtype=bool))
    seg = a_cum[..., :, None] - a_cum[..., None, :]
    decay = jnp.exp(jnp.where(causal, seg, -jnp.inf))
    cb = jnp.einsum('bclgn,bcsgn->bgcls', cc, bc)
    y_diag = jnp.einsum('bgcls,bghcls,bcsghp->bclghp', cb, decay, xdt)
    decay_states = jnp.exp(a_cum[..., -1:] - a_cum)
    states = jnp.einsum('bclgn,bghcl,bclghp->bcghpn', bc, decay_states, xdt)
    chunk_decay = jnp.exp(a_cum[..., -1])

    def step(carry, inp):
        dec, st = inp
        return carry * dec[..., None, None] + st, carry

    init = jnp.zeros((b, g, hg, p, n), f32)
    _, prev = lax.scan(step, init, (jnp.moveaxis(chunk_decay, -1, 0), jnp.moveaxis(states, 1, 0)))
    prev = jnp.moveaxis(prev, 0, 1)
    y_off = jnp.einsum('bclgn,bcghpn,bghcl->bclghp', cc, prev, jnp.exp(a_cum))
    return (y_diag + y_off).reshape(b, s, h, p)


def diff_attention(q, k, v, lam):
    b, s, h, _, d = q.shape
    nb = s // Q_BLOCK
    qb = jnp.moveaxis(q.reshape(b, nb, Q_BLOCK, h, 2, d), 1, 0)
    k_pos = jnp.arange(s)
    scale = d ** -0.5

    def block(args):
        qi, i = args
        sc = jnp.einsum('bqhcd,bkhcd->bhcqk', qi, k).astype(jnp.float32) * scale
        q_pos = i * Q_BLOCK + jnp.arange(Q_BLOCK)
        sc = jnp.where(k_pos[None, :] <= q_pos[:, None], sc, -jnp.inf)
        pr = jax.nn.softmax(sc, axis=-1)
        w = pr[:, :, 0] - lam.astype(jnp.float32) * pr[:, :, 1]
        return jnp.einsum('bhqk,bkhe->bqhe', w.astype(v.dtype), v)

    o = lax.map(block, (qb, jnp.arange(nb)))
    return jnp.moveaxis(o, 0, 1).reshape(b, s, h, v.shape[-1])


def hybrid_mixer(h, lambda_init, w_in, b_gate, ssd_conv_w, ssd_conv_b, ssd_dt_bias, ssd_a_log,
                 ssd_d, ssd_norm, ssd_wo, da_lq1, da_lk1, da_lq2, da_lk2, da_subln, da_wo,
                 cv_dw_w, cv_dw_b, cv_ln_g, cv_ln_b, cv_wo, w_out):
    b, s, _ = h.shape
    proj = h @ w_in
    gates, z, xbc, dt_raw, q, k, v, pw = jnp.split(proj, _split_points(IN_SIZES), axis=-1)
    gates = jax.nn.sigmoid(gates + b_gate).reshape(b, s, N_BRANCH, D_MODEL)

    xbc = jax.nn.silu(causal_dwconv(xbc, ssd_conv_w, ssd_conv_b))
    xs, bm, cm = jnp.split(xbc, [SSD_D_INNER, SSD_D_INNER + SSD_GROUPS * SSD_STATE], axis=-1)
    xs = xs.reshape(b, s, SSD_HEADS, SSD_HEAD_DIM)
    dt = jax.nn.softplus(dt_raw.astype(jnp.float32) + ssd_dt_bias.astype(jnp.float32))
    a_head = -jnp.exp(ssd_a_log.astype(jnp.float32))
    y = ssd_scan(xs, dt, a_head,
                 bm.reshape(b, s, SSD_GROUPS, SSD_STATE), cm.reshape(b, s, SSD_GROUPS, SSD_STATE))
    y = y.astype(h.dtype) + xs * ssd_d[:, None]
    y = gated_group_rms_norm(y.reshape(b, s, SSD_D_INNER), z, ssd_norm)
    y_a = y @ ssd_wo

    lam = (jnp.exp(jnp.sum(da_lq1.astype(jnp.float32) * da_lk1.astype(jnp.float32)))
           - jnp.exp(jnp.sum(da_lq2.astype(jnp.float32) * da_lk2.astype(jnp.float32))) + lambda_init)
    o = diff_attention(q.reshape(b, s, DA_HEADS, 2, DA_HEAD_DIM),
                       k.reshape(b, s, DA_HEADS, 2, DA_HEAD_DIM),
                       v.reshape(b, s, DA_HEADS, 2 * DA_HEAD_DIM), lam)
    o = rms_norm(o, da_subln, DA_EPS) * (1.0 - lambda_init)
    y_b = o.reshape(b, s, DA_V) @ da_wo

    pa, pg = jnp.split(pw, 2, axis=-1)
    u = pa * jax.nn.sigmoid(pg)
    u = causal_dwconv(u, cv_dw_w, cv_dw_b)
    u = jax.nn.silu(layer_norm(u, cv_ln_g, cv_ln_b))
    y_c = u @ cv_wo

    merged = gates[:, :, 0] * y_a + gates[:, :, 1] * y_b + gates[:, :, 2] * y_c
    return merged @ w_out


def setup_inputs(seed: int = 0) -> dict:
    key = jax.random.key(seed)
    ks = jax.random.split(key, 32)
    L, D, F = DEPTH, D_MODEL, D_FF

    def nrm(k, shape, scale):
        return jax.random.normal(k, shape, jnp.float32) * scale

    def gain(k, shape):
        return 1.0 + 0.02 * jax.random.normal(k, shape, jnp.float32)

    dt0 = jnp.exp(jax.random.uniform(ks[9], (L, SSD_HEADS), jnp.float32,
                                     math.log(1e-3), math.log(1e-1)))
    return {
        "x": jax.random.normal(ks[0], (BATCH, SEQ, D), jnp.float32),
        "ffn1_norm": gain(ks[1], (L, D)),
        "ffn1_w13": nrm(ks[2], (L, D, 2 * F), D ** -0.5),
        "ffn1_w2": nrm(ks[3], (L, F, D), F ** -0.5),
        "mix_norm": gain(ks[4], (L, D)),
        "w_in": nrm(ks[5], (L, D, IN_COLS), D ** -0.5),
        "b_gate": nrm(ks[6], (L, N_BRANCH * D), 0.02),
        "ssd_conv_w": nrm(ks[7], (L, SSD_CONV, SSD_XBC), SSD_CONV ** -0.5),
        "ssd_conv_b": nrm(ks[8], (L, SSD_XBC), 0.02),
        "ssd_dt_bias": dt0 + jnp.log(-jnp.expm1(-dt0)),
        "ssd_a_log": jnp.log(jax.random.uniform(ks[10], (L, SSD_HEADS), jnp.float32, 1.0, 16.0)),
        "ssd_d": 1.0 + 0.1 * jax.random.normal(ks[11], (L, SSD_HEADS), jnp.float32),
        "ssd_norm": gain(ks[12], (L, SSD_D_INNER)),
        "ssd_wo": nrm(ks[13], (L, SSD_D_INNER, D), SSD_D_INNER ** -0.5),
        "da_lq1": nrm(ks[14], (L, DA_HEAD_DIM), 0.1),
        "da_lk1": nrm(ks[15], (L, DA_HEAD_DIM), 0.1),
        "da_lq2": nrm(ks[16], (L, DA_HEAD_DIM), 0.1),
        "da_lk2": nrm(ks[17], (L, DA_HEAD_DIM), 0.1),
        "da_subln": gain(ks[18], (L, 2 * DA_HEAD_DIM)),
        "da_wo": nrm(ks[19], (L, DA_V, D), DA_V ** -0.5),
        "cv_dw_w": nrm(ks[20], (L, CONV_WIDTH, CONV_CH), CONV_WIDTH ** -0.5),
        "cv_dw_b": nrm(ks[21], (L, CONV_CH), 0.02),
        "cv_ln_g": gain(ks[22], (L, CONV_CH)),
        "cv_ln_b": nrm(ks[23], (L, CONV_CH), 0.02),
        "cv_wo": nrm(ks[24], (L, CONV_CH, D), CONV_CH ** -0.5),
        "w_out": nrm(ks[25], (L, D, D), D ** -0.5),
        "ffn2_norm": gain(ks[26], (L, D)),
        "ffn2_w13": nrm(ks[27], (L, D, 2 * F), D ** -0.5),
        "ffn2_w2": nrm(ks[28], (L, F, D), F ** -0.5),
        "final_norm": gain(ks[29], (D,)),
    }


def reference(x, ffn1_norm, ffn1_w13, ffn1_w2, mix_norm, w_in, b_gate, ssd_conv_w, ssd_conv_b,
              ssd_dt_bias, ssd_a_log, ssd_d, ssd_norm, ssd_wo, da_lq1, da_lk1, da_lq2, da_lk2,
              da_subln, da_wo, cv_dw_w, cv_dw_b, cv_ln_g, cv_ln_b, cv_wo, w_out,
              ffn2_norm, ffn2_w13, ffn2_w2, final_norm):
    for l in range(DEPTH):
        lambda_init = 0.8 - 0.6 * math.exp(-0.3 * l)
        x = x + 0.5 * swiglu(rms_norm(x, ffn1_norm[l]), ffn1_w13[l], ffn1_w2[l])
        x = x + hybrid_mixer(
            rms_norm(x, mix_norm[l]), lambda_init, w_in[l], b_gate[l],
            ssd_conv_w[l], ssd_conv_b[l], ssd_dt_bias[l], ssd_a_log[l], ssd_d[l], ssd_norm[l], ssd_wo[l],
            da_lq1[l], da_lk1[l], da_lq2[l], da_lk2[l], da_subln[l], da_wo[l],
            cv_dw_w[l], cv_dw_b[l], cv_ln_g[l], cv_ln_b[l], cv_wo[l], w_out[l])
        x = x + 0.5 * swiglu(rms_norm(x, ffn2_norm[l]), ffn2_w13[l], ffn2_w2[l])
    return rms_norm(x, final_norm)
```

```python
import functools
import math

import jax
import jax.numpy as jnp
from jax import lax
from jax.experimental import pallas as pl
from jax.experimental.pallas import tpu as pltpu

F32 = jnp.float32
BF16 = jnp.bfloat16

D_MODEL = 1024
D_FF = 2816
EPS = 1e-6
SSD_HEADS = 16
SSD_HEAD_DIM = 64
SSD_GROUPS = 2
SSD_STATE = 128
SSD_CONV = 4
SSD_CHUNK = 128
SSD_XBC = 1536
DA_HEADS = 8
DA_HEAD_DIM = 64
DA_EPS = 1e-5
LN_EPS = 1e-5
CONV_WIDTH = 31
IN_SIZES = (3 * D_MODEL, D_MODEL, SSD_XBC, SSD_HEADS, D_MODEL, D_MODEL, D_MODEL, 2 * D_MODEL)

LANES = 128
SUBLANES = 8
VMEM_LIMIT_BYTES = 56 * 1024 * 1024

P_GATES = 0
P_Z = 3072
P_PW = 4096
P_Q = 6144
P_K = 7168
P_V = 8192
P_XBC = 9216
P_COLS = 10752


def _params(*sem):
    return pltpu.CompilerParams(dimension_semantics=sem, vmem_limit_bytes=VMEM_LIMIT_BYTES)


def _resident(shape):
    nd = len(shape)
    return pl.BlockSpec(shape, lambda *_: (0,) * nd, pipeline_mode=pl.Buffered(1))


def _sigmoid(x):
    return 1.0 / (1.0 + jnp.exp(-x))


def _silu(x):
    return x * _sigmoid(x)


def _ffn_kernel(*refs, f_chunk, mode):
    if mode == "mix":
        x_ref, g_ref, w13_ref, w2_ref, gm_ref, wdt_ref, xo_ref, h_ref, dt_ref = refs
    elif mode == "final":
        x_ref, g_ref, w13_ref, w2_ref, gf_ref, xo_ref = refs
    else:
        x_ref, g_ref, w13_ref, w2_ref, xo_ref = refs
    x = x_ref[...]
    h = (x * lax.rsqrt(jnp.mean(x * x, axis=-1, keepdims=True) + EPS) * g_ref[...]).astype(BF16)
    acc = jnp.zeros(x.shape, F32)
    for c in range(D_FF // f_chunk):
        lo = c * f_chunk
        gate = jnp.dot(h, w13_ref[:, lo:lo + f_chunk], preferred_element_type=F32)
        up = jnp.dot(h, w13_ref[:, D_FF + lo:D_FF + lo + f_chunk], preferred_element_type=F32)
        act = (_silu(gate) * up).astype(BF16)
        acc = acc + jnp.dot(act, w2_ref[lo:lo + f_chunk, :], preferred_element_type=F32)
    xn = x + 0.5 * acc
    if mode == "final":
        xo_ref[...] = xn * lax.rsqrt(jnp.mean(xn * xn, axis=-1, keepdims=True) + EPS) * gf_ref[...]
        return
    xo_ref[...] = xn
    if mode == "mix":
        hm = (xn * lax.rsqrt(jnp.mean(xn * xn, axis=-1, keepdims=True) + EPS) * gm_ref[...]).astype(BF16)
        h_ref[...] = hm
        dt_ref[...] = jnp.dot(hm, wdt_ref[...], preferred_element_type=F32)


def _ffn(x, g, w13, w2, *, mode, extra=(), tm=512, f_chunk=1408):
    t, d = x.shape
    row = lambda i: (i, 0)
    in_specs = [pl.BlockSpec((tm, d), row), _resident((1, d)), _resident(w13.shape), _resident(w2.shape)]
    out_shape = [jax.ShapeDtypeStruct((t, d), F32)]
    out_specs = [pl.BlockSpec((tm, d), row)]
    if mode == "mix":
        gm, wdt = extra
        in_specs += [_resident((1, d)), _resident(wdt.shape)]
        out_shape += [jax.ShapeDtypeStruct((t, d), BF16), jax.ShapeDtypeStruct((t, LANES), F32)]
        out_specs += [pl.BlockSpec((tm, d), row), pl.BlockSpec((tm, LANES), row)]
    elif mode == "final":
        in_specs += [_resident((1, d))]
    out = pl.pallas_call(
        functools.partial(_ffn_kernel, f_chunk=f_chunk, mode=mode),
        grid=(t // tm,),
        in_specs=in_specs,
        out_specs=out_specs,
        out_shape=out_shape,
        compiler_params=_params("parallel"),
    )(x, g, w13, w2, *extra)
    return out if mode == "mix" else out[0]


def _proj_kernel(h_ref, w_ref, o_ref):
    o_ref[...] = jnp.dot(h_ref[...], w_ref[...], preferred_element_type=F32).astype(o_ref.dtype)


def _proj(h, w, *, tm=1024, tn=1792):
    t, d = h.shape
    n = w.shape[1]
    return pl.pallas_call(
        _proj_kernel,
        grid=(n // tn, t // tm),
        in_specs=[pl.BlockSpec((tm, d), lambda j, i: (i, 0)), pl.BlockSpec((d, tn), lambda j, i: (0, j))],
        out_specs=pl.BlockSpec((tm, tn), lambda j, i: (i, j)),
        out_shape=jax.ShapeDtypeStruct((t, n), BF16),
        compiler_params=_params("parallel", "parallel"),
    )(h, w)


def _ssd_kernel(xbc_ref, z_ref, dt_ref, cw_ref, cb_ref, dtb_ref, aneg_ref, dexp_ref, nrm_ref, expand_ref,
                o_ref, xwin_ref, state_ref):
    c = pl.program_id(1)
    L = SSD_CHUNK
    tail = SUBLANES
    hi = lax.Precision.HIGHEST

    @pl.when(c == 0)
    def _():
        xwin_ref[0:tail, :] = jnp.zeros((tail, SSD_XBC), F32)
        state_ref[...] = jnp.zeros(state_ref.shape, F32)

    @pl.when(c != 0)
    def _():
        xwin_ref[0:tail, :] = xwin_ref[L:L + tail, :]

    xwin_ref[tail:tail + L, :] = xbc_ref[...].astype(F32)
    conv = cb_ref[...]
    for j in range(SSD_CONV):
        off = tail - (SSD_CONV - 1) + j
        conv = conv + cw_ref[j:j + 1, :] * xwin_ref[off:off + L, :]
    xbc = _silu(conv)
    xs = xbc[:, :D_MODEL]

    dt_in = dt_ref[...] + dtb_ref[...]
    dt = jnp.maximum(dt_in, 0.0) + jnp.log(1.0 + jnp.exp(-jnp.abs(dt_in)))
    a = dt * aneg_ref[...]
    row = lax.broadcasted_iota(jnp.int32, (L, L), 0)
    col = lax.broadcasted_iota(jnp.int32, (L, L), 1)
    causal = col <= row
    tril = jnp.where(causal, 1.0, 0.0).astype(F32)
    a_cum = jnp.dot(tril, a, precision=hi, preferred_element_type=F32)
    a_cum_t = a_cum.T
    a_last = a_cum[L - 1:L, :]
    exp_a = jnp.exp(a_cum)
    dstate = jnp.exp(a_last - a_cum)
    expand = expand_ref[...]
    dt_x = jnp.dot(dt, expand, precision=hi, preferred_element_type=F32)
    dtd_x = jnp.dot(dt * dstate, expand, precision=hi, preferred_element_type=F32)
    expa_x = jnp.dot(exp_a, expand, precision=hi, preferred_element_type=F32)

    xdt = xs * dt_x
    xdt_state = (xs * dtd_x).astype(BF16)
    lane = lax.broadcasted_iota(jnp.int32, (L, D_MODEL), 1)
    lo_half = (lane & (LANES - 1)) < SSD_HEAD_DIM
    xdt_lo = jnp.where(lo_half, xdt, 0.0).astype(BF16)
    xdt_hi = jnp.where(lo_half, 0.0, xdt).astype(BF16)

    gw = SSD_STATE
    cg = D_MODEL // SSD_GROUPS
    hg = SSD_HEADS // SSD_GROUPS
    y_parts = []
    for g in range(SSD_GROUPS):
        bm = xbc[:, D_MODEL + g * gw:D_MODEL + (g + 1) * gw]
        cm = xbc[:, D_MODEL + SSD_GROUPS * gw + g * gw:D_MODEL + SSD_GROUPS * gw + (g + 1) * gw]
        bm16 = bm.astype(BF16)
        cm16 = cm.astype(BF16)
        cb = lax.dot_general(cm16, bm16, (((1,), (1,)), ((), ())), preferred_element_type=F32)
        for pair in range(hg // 2):
            ms = []
            for k in range(2):
                h = g * hg + pair * 2 + k
                seg = a_cum[:, h:h + 1] - a_cum_t[h:h + 1, :]
                ms.append((cb * jnp.exp(jnp.where(causal, seg, -jnp.inf))).astype(BF16))
            m2 = jnp.concatenate(ms, axis=1)
            c0 = (g * hg + pair * 2) * SSD_HEAD_DIM
            rhs = jnp.concatenate([xdt_lo[:, c0:c0 + LANES], xdt_hi[:, c0:c0 + LANES]], axis=0)
            y_parts.append(jnp.dot(m2, rhs, preferred_element_type=F32))
    y = jnp.concatenate(y_parts, axis=1)

    outs = []
    for g in range(SSD_GROUPS):
        bm = xbc[:, D_MODEL + g * gw:D_MODEL + (g + 1) * gw]
        cm = xbc[:, D_MODEL + SSD_GROUPS * gw + g * gw:D_MODEL + SSD_GROUPS * gw + (g + 1) * gw]
        prev = state_ref[g]
        y_off = jnp.dot(cm.astype(BF16), prev.astype(BF16), preferred_element_type=F32)
        new = jnp.dot(bm.T.astype(BF16), xdt_state[:, g * cg:(g + 1) * cg], preferred_element_type=F32)
        state_ref[g] = prev * expa_x[L - 1:L, g * cg:(g + 1) * cg] + new
        yg = y[:, g * cg:(g + 1) * cg] + y_off * expa_x[:, g * cg:(g + 1) * cg]
        yg = yg + xs[:, g * cg:(g + 1) * cg] * dexp_ref[:, g * cg:(g + 1) * cg]
        u = yg * _silu(z_ref[:, g * cg:(g + 1) * cg].astype(F32))
        u = u * lax.rsqrt(jnp.mean(u * u, axis=-1, keepdims=True) + EPS)
        outs.append(u * nrm_ref[:, g * cg:(g + 1) * cg])
    o_ref[...] = jnp.concatenate(outs, axis=1).astype(o_ref.dtype)


def _ssd(p, dt_raw, cw, cb, dtb, aneg, dexp, nrm, expand, *, batch, seq):
    t = batch * seq
    nc = seq // SSD_CHUNK
    L = SSD_CHUNK
    rowmap = lambda b, c: (b * nc + c, 0)
    return pl.pallas_call(
        _ssd_kernel,
        grid=(batch, nc),
        in_specs=[
            pl.BlockSpec((L, SSD_XBC), lambda b, c: (b * nc + c, P_XBC // SSD_XBC)),
            pl.BlockSpec((L, D_MODEL), lambda b, c: (b * nc + c, P_Z // D_MODEL)),
            pl.BlockSpec((L, LANES), rowmap),
            _resident(cw.shape), _resident(cb.shape), _resident(dtb.shape), _resident(aneg.shape),
            _resident(dexp.shape), _resident(nrm.shape), _resident(expand.shape),
        ],
        out_specs=pl.BlockSpec((L, D_MODEL), rowmap),
        out_shape=jax.ShapeDtypeStruct((t, D_MODEL), BF16),
        scratch_shapes=[
            pltpu.VMEM((L + 2 * SUBLANES, SSD_XBC), F32),
            pltpu.VMEM((SSD_GROUPS, SSD_STATE, D_MODEL // SSD_GROUPS), F32),
        ],
        compiler_params=_params("parallel", "arbitrary"),
    )(p, p, dt_raw, cw, cb, dtb, aneg, dexp, nrm, expand)


def _attn_kernel(q_ref, k_ref, v_ref, lam_ref, sub_ref, o_ref, m_ref, l_ref, acc_ref, *, tq, lambda_init):
    i = pl.program_id(2)
    q = q_ref[...]
    lane = lax.broadcasted_iota(jnp.int32, (tq, LANES), 1)
    first = lane < DA_HEAD_DIM
    m_ref[...] = jnp.full(m_ref.shape, -jnp.inf, F32)
    l_ref[...] = jnp.zeros(l_ref.shape, F32)
    acc_ref[...] = jnp.zeros(acc_ref.shape, F32)
    nt = (((1,), (1,)), ((), ()))

    def step(j, masked):
        start = pl.multiple_of(j * tq, tq)
        k = k_ref[pl.ds(start, tq), :]
        v = v_ref[pl.ds(start, tq), :]
        zero = jnp.zeros_like(k)
        for c in range(2):
            kc = jnp.where(first, k, zero) if c == 0 else jnp.where(first, zero, k)
            s = lax.dot_general(q, kc, nt, preferred_element_type=F32)
            if masked:
                r = lax.broadcasted_iota(jnp.int32, (tq, tq), 0)
                cc = lax.broadcasted_iota(jnp.int32, (tq, tq), 1)
                s = jnp.where(cc <= r, s, -jnp.inf)
            m_old = m_ref[c]
            m_new = jnp.maximum(m_old, jnp.max(s, axis=-1, keepdims=True))
            alpha = jnp.exp(m_old - m_new)
            p = jnp.exp(s - m_new)
            l_ref[c] = alpha * l_ref[c] + jnp.sum(p, axis=-1, keepdims=True)
            acc_ref[c] = alpha * acc_ref[c] + jnp.dot(p.astype(BF16), v, preferred_element_type=F32)
            m_ref[c] = m_new

    def body(j, carry):
        step(j, False)
        return carry

    lax.fori_loop(0, i, body, 0)
    step(i, True)

    lp = lam_ref[...]
    lam = (jnp.exp(jnp.sum(lp[0:1] * lp[1:2], axis=-1, keepdims=True))
           - jnp.exp(jnp.sum(lp[2:3] * lp[3:4], axis=-1, keepdims=True)) + lambda_init)
    o = acc_ref[0] / l_ref[0] - lam * (acc_ref[1] / l_ref[1])
    o = o * lax.rsqrt(jnp.mean(o * o, axis=-1, keepdims=True) + DA_EPS) * sub_ref[...]
    o_ref[...] = (o * (1.0 - lambda_init)).astype(o_ref.dtype)


def _attn(p, lam_rows, subln, *, batch, seq, lambda_init, tq=256):
    t = batch * seq
    nq = seq // tq
    return pl.pallas_call(
        functools.partial(_attn_kernel, tq=tq, lambda_init=lambda_init),
        grid=(batch, DA_HEADS, nq),
        in_specs=[
            pl.BlockSpec((tq, LANES), lambda b, h, i: (b * nq + i, P_Q // LANES + h)),
            pl.BlockSpec((seq, LANES), lambda b, h, i: (b, P_K // LANES + h)),
            pl.BlockSpec((seq, LANES), lambda b, h, i: (b, P_V // LANES + h)),
            _resident(lam_rows.shape), _resident(subln.shape),
        ],
        out_specs=pl.BlockSpec((tq, LANES), lambda b, h, i: (b * nq + i, h)),
        out_shape=jax.ShapeDtypeStruct((t, D_MODEL), BF16),
        scratch_shapes=[
            pltpu.VMEM((2, tq, 1), F32), pltpu.VMEM((2, tq, 1), F32), pltpu.VMEM((2, tq, LANES), F32),
        ],
        compiler_params=_params("parallel", "parallel", "arbitrary"),
    )(p, p, p, lam_rows, subln)


def _conv_kernel(pw_ref, w_ref, b_ref, g_ref, beta_ref, o_ref, win_ref, *, tm, rows):
    i = pl.program_id(1)
    halo = 4 * SUBLANES

    @pl.when(i == 0)
    def _():
        win_ref[0:halo, :] = jnp.zeros((halo, D_MODEL), F32)

    @pl.when(i != 0)
    def _():
        win_ref[0:halo, :] = win_ref[tm:tm + halo, :]

    pa = pw_ref[:, :D_MODEL].astype(F32)
    pg = pw_ref[:, D_MODEL:].astype(F32)
    win_ref[halo:halo + tm, :] = pa * _sigmoid(pg)
    for r0 in range(0, tm, rows):
        acc = jnp.broadcast_to(b_ref[...], (rows, D_MODEL))
        for j in range(CONV_WIDTH):
            off = halo - (CONV_WIDTH - 1) + j + r0
            acc = acc + w_ref[j:j + 1, :] * win_ref[off:off + rows, :]
        mu = jnp.mean(acc, axis=-1, keepdims=True)
        d = acc - mu
        var = jnp.mean(d * d, axis=-1, keepdims=True)
        y = d * lax.rsqrt(var + LN_EPS) * g_ref[...] + beta_ref[...]
        o_ref[r0:r0 + rows, :] = _silu(y).astype(o_ref.dtype)


def _conv(p, w, b, g, beta, *, batch, seq, tm=256, rows=32):
    t = batch * seq
    nt = seq // tm
    return pl.pallas_call(
        functools.partial(_conv_kernel, tm=tm, rows=rows),
        grid=(batch, nt),
        in_specs=[
            pl.BlockSpec((tm, 2 * D_MODEL), lambda bb, i: (bb * nt + i, P_PW // (2 * D_MODEL))),
            _resident(w.shape), _resident(b.shape), _resident(g.shape), _resident(beta.shape),
        ],
        out_specs=pl.BlockSpec((tm, D_MODEL), lambda bb, i: (bb * nt + i, 0)),
        out_shape=jax.ShapeDtypeStruct((t, D_MODEL), BF16),
        scratch_shapes=[pltpu.VMEM((tm + 4 * SUBLANES, D_MODEL), F32)],
        compiler_params=_params("parallel", "arbitrary"),
    )(p, w, b, g, beta)


def _merge_kernel(x_ref, ua_ref, ub_ref, uc_ref, gt_ref, bg_ref, wa_ref, wb_ref, wc_ref, wo_ref, o_ref):
    d = D_MODEL
    merged = None
    for n, (u_ref, w_ref) in enumerate(((ua_ref, wa_ref), (ub_ref, wb_ref), (uc_ref, wc_ref))):
        y = jnp.dot(u_ref[...], w_ref[...], preferred_element_type=F32)
        gate = _sigmoid(gt_ref[:, n * d:(n + 1) * d].astype(F32) + bg_ref[:, n * d:(n + 1) * d])
        merged = gate * y if merged is None else merged + gate * y
    o_ref[...] = x_ref[...] + jnp.dot(merged.astype(BF16), wo_ref[...], preferred_element_type=F32)


def _merge(x, ua, ub, uc, p, bg, wa, wb, wc, wo, *, tm=512):
    t, d = x.shape
    row = lambda i: (i, 0)
    act = pl.BlockSpec((tm, d), row)
    return pl.pallas_call(
        _merge_kernel,
        grid=(t // tm,),
        in_specs=[act, act, act, act, pl.BlockSpec((tm, 3 * d), lambda i: (i, P_GATES // (3 * d))),
                  _resident(bg.shape), _resident(wa.shape), _resident(wb.shape), _resident(wc.shape),
                  _resident(wo.shape)],
        out_specs=act,
        out_shape=jax.ShapeDtypeStruct((t, d), F32),
        compiler_params=_params("parallel"),
    )(x, ua, ub, uc, p, bg, wa, wb, wc, wo)


def _split_w_in(w):
    pts = [0]
    for s in IN_SIZES:
        pts.append(pts[-1] + s)
    gates, z, xbc, dt, q, k, v, pw = (w[:, pts[n]:pts[n + 1]] for n in range(len(IN_SIZES)))
    q = q * (DA_HEAD_DIM ** -0.5)
    w_main = jnp.concatenate([gates, z, pw, q, k, v, xbc], axis=1).astype(BF16)
    w_dt = jnp.pad(dt, ((0, 0), (0, LANES - SSD_HEADS))).astype(BF16)
    return w_main, w_dt


def _pad_lanes(v):
    return jnp.pad(v, (0, LANES - v.shape[0]))[None, :]


def kernel(x, ffn1_norm, ffn1_w13, ffn1_w2, mix_norm, w_in, b_gate, ssd_conv_w, ssd_conv_b, ssd_dt_bias,
           ssd_a_log, ssd_d, ssd_norm, ssd_wo, da_lq1, da_lk1, da_lq2, da_lk2, da_subln, da_wo, cv_dw_w,
           cv_dw_b, cv_ln_g, cv_ln_b, cv_wo, w_out, ffn2_norm, ffn2_w13, ffn2_w2, final_norm):
    batch, seq, d = x.shape
    depth = ffn1_norm.shape[0]
    xf = x.reshape(batch * seq, d)
    head_of_channel = jnp.arange(D_MODEL, dtype=jnp.int32) // SSD_HEAD_DIM
    expand = (jnp.arange(LANES, dtype=jnp.int32)[:, None] == head_of_channel[None, :]).astype(F32)
    for l in range(depth):
        lambda_init = 0.8 - 0.6 * math.exp(-0.3 * l)
        w_main, w_dt = _split_w_in(w_in[l])
        xf, h, dt_raw = _ffn(xf, ffn1_norm[l][None, :], ffn1_w13[l].astype(BF16), ffn1_w2[l].astype(BF16),
                             mode="mix", extra=(mix_norm[l][None, :], w_dt))
        p = _proj(h, w_main)
        ua = _ssd(p, dt_raw, ssd_conv_w[l], ssd_conv_b[l][None, :], _pad_lanes(ssd_dt_bias[l]),
                  _pad_lanes(-jnp.exp(ssd_a_log[l])), jnp.repeat(ssd_d[l], SSD_HEAD_DIM)[None, :],
                  ssd_norm[l][None, :], expand, batch=batch, seq=seq)
        lam_rows = jnp.pad(jnp.stack([da_lq1[l], da_lk1[l], da_lq2[l], da_lk2[l]]),
                           ((0, SUBLANES - 4), (0, LANES - DA_HEAD_DIM)))
        ub = _attn(p, lam_rows, da_subln[l][None, :], batch=batch, seq=seq, lambda_init=lambda_init)
        uc = _conv(p, cv_dw_w[l], cv_dw_b[l][None, :], cv_ln_g[l][None, :], cv_ln_b[l][None, :],
                   batch=batch, seq=seq)
        xf = _merge(xf, ua, ub, uc, p, b_gate[l][None, :], ssd_wo[l].astype(BF16), da_wo[l].astype(BF16),
                    cv_wo[l].astype(BF16), w_out[l].astype(BF16))
        if l == depth - 1:
            xf = _ffn(xf, ffn2_norm[l][None, :], ffn2_w13[l].astype(BF16), ffn2_w2[l].astype(BF16),
                      mode="final", extra=(final_norm[None, :],))
        else:
            xf = _ffn(xf, ffn2_norm[l][None, :], ffn2_w13[l].astype(BF16), ffn2_w2[l].astype(BF16),
                      mode="plain")
    return xf.reshape(batch, seq, d)
```

```python
import functools
import math

import jax
import jax.numpy as jnp
from jax import lax
from jax.experimental import pallas as pl
from jax.experimental.pallas import tpu as pltpu

F32 = jnp.float32
BF16 = jnp.bfloat16

D_MODEL = 1024
D_FF = 2816
EPS = 1e-6
SSD_HEADS = 16
SSD_HEAD_DIM = 64
SSD_GROUPS = 2
SSD_STATE = 128
SSD_CONV = 4
SSD_CHUNK = 128
SSD_XBC = 1536
DA_HEADS = 8
DA_HEAD_DIM = 64
DA_EPS = 1e-5
LN_EPS = 1e-5
CONV_WIDTH = 31
IN_SIZES = (3 * D_MODEL, D_MODEL, SSD_XBC, SSD_HEADS, D_MODEL, D_MODEL, D_MODEL, 2 * D_MODEL)

LANES = 128
SUBLANES = 8
VMEM_LIMIT_BYTES = 56 * 1024 * 1024

P_GATES = 0
P_Z = 3072
P_PW = 4096
P_Q = 6144
P_K = 7168
P_V = 8192
P_XBC = 9216
P_COLS = 10752


def _params(*sem):
    return pltpu.CompilerParams(dimension_semantics=sem, vmem_limit_bytes=VMEM_LIMIT_BYTES)


def _resident(shape):
    nd = len(shape)
    return pl.BlockSpec(shape, lambda *_: (0,) * nd, pipeline_mode=pl.Buffered(1))


def _sigmoid(x):
    return 1.0 / (1.0 + jnp.exp(-x))


def _silu(x):
    return x * _sigmoid(x)


def _ffn_kernel(*refs, f_chunk, mode):
    if mode == "mix":
        x_ref, g_ref, w13_ref, w2_ref, gm_ref, wdt_ref, xo_ref, h_ref, dt_ref = refs
    elif mode == "final":
        x_ref, g_ref, w13_ref, w2_ref, gf_ref, xo_ref = refs
    else:
        x_ref, g_ref, w13_ref, w2_ref, xo_ref = refs
    x = x_ref[...]
    h = (x * lax.rsqrt(jnp.mean(x * x, axis=-1, keepdims=True) + EPS) * g_ref[...]).astype(BF16)
    acc = jnp.zeros(x.shape, F32)
    for c in range(D_FF // f_chunk):
        lo = c * f_chunk
        gate = jnp.dot(h, w13_ref[:, lo:lo + f_chunk], preferred_element_type=F32)
        up = jnp.dot(h, w13_ref[:, D_FF + lo:D_FF + lo + f_chunk], preferred_element_type=F32)
        act = (_silu(gate) * up).astype(BF16)
        acc = acc + jnp.dot(act, w2_ref[lo:lo + f_chunk, :], preferred_element_type=F32)
    xn = x + 0.5 * acc
    if mode == "final":
        xo_ref[...] = xn * lax.rsqrt(jnp.mean(xn * xn, axis=-1, keepdims=True) + EPS) * gf_ref[...]
        return
    xo_ref[...] = xn
    if mode == "mix":
        hm = (xn * lax.rsqrt(jnp.mean(xn * xn, axis=-1, keepdims=True) + EPS) * gm_ref[...]).astype(BF16)
        h_ref[...] = hm
        dt_ref[...] = jnp.dot(hm, wdt_ref[...], preferred_element_type=F32)


def _ffn(x, g, w13, w2, *, mode, extra=(), tm=512, f_chunk=1408):
    t, d = x.shape
    row = lambda i: (i, 0)
    in_specs = [pl.BlockSpec((tm, d), row), _resident((1, d)), _resident(w13.shape), _resident(w2.shape)]
    out_shape = [jax.ShapeDtypeStruct((t, d), F32)]
    out_specs = [pl.BlockSpec((tm, d), row)]
    if mode == "mix":
        gm, wdt = extra
        in_specs += [_resident((1, d)), _resident(wdt.shape)]
        out_shape += [jax.ShapeDtypeStruct((t, d), BF16), jax.ShapeDtypeStruct((t, LANES), F32)]
        out_specs += [pl.BlockSpec((tm, d), row), pl.BlockSpec((tm, LANES), row)]
    elif mode == "final":
        in_specs += [_resident((1, d))]
    out = pl.pallas_call(
        functools.partial(_ffn_kernel, f_chunk=f_chunk, mode=mode),
        grid=(t // tm,),
        in_specs=in_specs,
        out_specs=out_specs,
        out_shape=out_shape,
        compiler_params=_params("parallel"),
    )(x, g, w13, w2, *extra)
    return out if mode == "mix" else out[0]


def _proj_kernel(h_ref, w_ref, o_ref):
    o_ref[...] = jnp.dot(h_ref[...], w_ref[...], preferred_element_type=F32).astype(o_ref.dtype)


def _proj(h, w, *, tm=1024, tn=1792):
    t, d = h.shape
    n = w.shape[1]
    return pl.pallas_call(
        _proj_kernel,
        grid=(n // tn, t // tm),
        in_specs=[pl.BlockSpec((tm, d), lambda j, i: (i, 0)), pl.BlockSpec((d, tn), lambda j, i: (0, j))],
        out_specs=pl.BlockSpec((tm, tn), lambda j, i: (i, j)),
        out_shape=jax.ShapeDtypeStruct((t, n), BF16),
        compiler_params=_params("parallel", "parallel"),
    )(h, w)


def _ssd_kernel(xbc_ref, z_ref, dt_ref, cw_ref, cb_ref, dtb_ref, aneg_ref, dexp_ref, nrm_ref, expand_ref,
                o_ref, xwin_ref, state_ref):
    c = pl.program_id(1)
    L = SSD_CHUNK
    tail = SUBLANES
    hi = lax.Precision.HIGHEST

    @pl.when(c == 0)
    def _():
        xwin_ref[0:tail, :] = jnp.zeros((tail, SSD_XBC), F32)
        state_ref[...] = jnp.zeros(state_ref.shape, F32)

    @pl.when(c != 0)
    def _():
        xwin_ref[0:tail, :] = xwin_ref[L:L + tail, :]

    xwin_ref[tail:tail + L, :] = xbc_ref[...].astype(F32)
    conv = cb_ref[...]
    for j in range(SSD_CONV):
        off = tail - (SSD_CONV - 1) + j
        conv = conv + cw_ref[j:j + 1, :] * xwin_ref[off:off + L, :]
    xbc = _silu(conv)
    xs = xbc[:, :D_MODEL]

    dt_in = dt_ref[...] + dtb_ref[...]
    dt = jnp.maximum(dt_in, 0.0) + jnp.log(1.0 + jnp.exp(-jnp.abs(dt_in)))
    a = dt * aneg_ref[...]
    row = lax.broadcasted_iota(jnp.int32, (L, L), 0)
    col = lax.broadcasted_iota(jnp.int32, (L, L), 1)
    causal = col <= row
    tril = jnp.where(causal, 1.0, 0.0).astype(F32)
    a_cum = jnp.dot(tril, a, precision=hi, preferred_element_type=F32)
    a_cum_t = a_cum.T
    a_last = a_cum[L - 1:L, :]
    exp_a = jnp.exp(a_cum)
    dstate = jnp.exp(a_last - a_cum)
    expand = expand_ref[...]
    dt_x = jnp.dot(dt, expand, precision=hi, preferred_element_type=F32)
    dtd_x = jnp.dot(dt * dstate, expand, precision=hi, preferred_element_type=F32)
    expa_x = jnp.dot(exp_a, expand, precision=hi, preferred_element_type=F32)

    xdt = xs * dt_x
    xdt_state = (xs * dtd_x).astype(BF16)
    lane = lax.broadcasted_iota(jnp.int32, (L, D_MODEL), 1)
    lo_half = (lane & (LANES - 1)) < SSD_HEAD_DIM
    xdt_lo = jnp.where(lo_half, xdt, 0.0).astype(BF16)
    xdt_hi = jnp.where(lo_half, 0.0, xdt).astype(BF16)

    gw = SSD_STATE
    cg = D_MODEL // SSD_GROUPS
    hg = SSD_HEADS // SSD_GROUPS
    y_parts = []
    for g in range(SSD_GROUPS):
        bm = xbc[:, D_MODEL + g * gw:D_MODEL + (g + 1) * gw]
        cm = xbc[:, D_MODEL + SSD_GROUPS * gw + g * gw:D_MODEL + SSD_GROUPS * gw + (g + 1) * gw]
        bm16 = bm.astype(BF16)
        cm16 = cm.astype(BF16)
        cb = lax.dot_general(cm16, bm16, (((1,), (1,)), ((), ())), preferred_element_type=F32)
        for pair in range(hg // 2):
            ms = []
            for k in range(2):
                h = g * hg + pair * 2 + k
                seg = a_cum[:, h:h + 1] - a_cum_t[h:h + 1, :]
                ms.append((cb * jnp.exp(jnp.where(causal, seg, -jnp.inf))).astype(BF16))
            m2 = jnp.concatenate(ms, axis=1)
            c0 = (g * hg + pair * 2) * SSD_HEAD_DIM
            rhs = jnp.concatenate([xdt_lo[:, c0:c0 + LANES], xdt_hi[:, c0:c0 + LANES]], axis=0)
            y_parts.append(jnp.dot(m2, rhs, preferred_element_type=F32))
    y = jnp.concatenate(y_parts, axis=1)

    outs = []
    for g in range(SSD_GROUPS):
        bm = xbc[:, D_MODEL + g * gw:D_MODEL + (g + 1) * gw]
        cm = xbc[:, D_MODEL + SSD_GROUPS * gw + g * gw:D_MODEL + SSD_GROUPS * gw + (g + 1) * gw]
        prev = state_ref[g]
        y_off = jnp.dot(cm.astype(BF16), prev.astype(BF16), preferred_element_type=F32)
        new = jnp.dot(bm.T.astype(BF16), xdt_state[:, g * cg:(g + 1) * cg], preferred_element_type=F32)
        state_ref[g] = prev * expa_x[L - 1:L, g * cg:(g + 1) * cg] + new
        yg = y[:, g * cg:(g + 1) * cg] + y_off * expa_x[:, g * cg:(g + 1) * cg]
        yg = yg + xs[:, g * cg:(g + 1) * cg] * dexp_ref[:, g * cg:(g + 1) * cg]
        u = yg * _silu(z_ref[:, g * cg:(g + 1) * cg].astype(F32))
        u = u * lax.rsqrt(jnp.mean(u * u, axis=-1, keepdims=True) + EPS)
        outs.append(u * nrm_ref[:, g * cg:(g + 1) * cg])
    o_ref[...] = jnp.concatenate(outs, axis=1).astype(o_ref.dtype)


def _ssd(p, dt_raw, cw, cb, dtb, aneg, dexp, nrm, expand, *, batch, seq):
    t = batch * seq
    nc = seq // SSD_CHUNK
    L = SSD_CHUNK
    rowmap = lambda b, c: (b * nc + c, 0)
    return pl.pallas_call(
        _ssd_kernel,
        grid=(batch, nc),
        in_specs=[
            pl.BlockSpec((L, SSD_XBC), lambda b, c: (b * nc + c, P_XBC // SSD_XBC)),
            pl.BlockSpec((L, D_MODEL), lambda b, c: (b * nc + c, P_Z // D_MODEL)),
            pl.BlockSpec((L, LANES), rowmap),
            _resident(cw.shape), _resident(cb.shape), _resident(dtb.shape), _resident(aneg.shape),
            _resident(dexp.shape), _resident(nrm.shape), _resident(expand.shape),
        ],
        out_specs=pl.BlockSpec((L, D_MODEL), rowmap),
        out_shape=jax.ShapeDtypeStruct((t, D_MODEL), BF16),
        scratch_shapes=[
            pltpu.VMEM((L + 2 * SUBLANES, SSD_XBC), F32),
            pltpu.VMEM((SSD_GROUPS, SSD_STATE, D_MODEL // SSD_GROUPS), F32),
        ],
        compiler_params=_params("parallel", "arbitrary"),
    )(p, p, dt_raw, cw, cb, dtb, aneg, dexp, nrm, expand)


def _attn_kernel(q_ref, k_ref, v_ref, lam_ref, sub_ref, o_ref, q2_ref, m_ref, acc_ref, s0_ref, s1_ref, *,
                 tq, tk, lambda_init):
    i = pl.program_id(2)
    q = q_ref[...]
    first = lax.broadcasted_iota(jnp.int32, (tq, LANES), 1) < DA_HEAD_DIM
    zero = jnp.zeros_like(q)
    q2_ref[0:tq, :] = jnp.where(first, q, zero)
    q2_ref[tq:2 * tq, :] = jnp.where(first, zero, q)
    m_ref[...] = jnp.full(m_ref.shape, -jnp.inf, F32)
    acc_ref[...] = jnp.zeros(acc_ref.shape, F32)
    ones = jnp.ones((tk, LANES), BF16)
    nt = (((1,), (1,)), ((), ()))

    s_refs = (s0_ref, s1_ref)

    def scores(j, dst_ref):
        start = pl.multiple_of(j * tk, tk)
        dst_ref[...] = lax.dot_general(q2_ref[...], k_ref[pl.ds(start, tk), :], nt,
                                       preferred_element_type=F32)

    def consume(j, src_ref, diag):
        start = pl.multiple_of(j * tk, tk)
        v2 = jnp.concatenate([v_ref[pl.ds(start, tk), :], ones], axis=1)
        s = src_ref[...]
        if diag is not None:
            r = lax.broadcasted_iota(jnp.int32, s.shape, 0) & (tq - 1)
            cc = lax.broadcasted_iota(jnp.int32, s.shape, 1) + diag
            s = jnp.where(cc <= r, s, -jnp.inf)
        m_old = m_ref[...]
        m_new = jnp.maximum(m_old, jnp.max(s, axis=-1, keepdims=True))
        alpha = jnp.exp2(m_old - m_new)
        p = jnp.exp2(s - jnp.concatenate([m_new] * (tk // LANES), axis=1))
        acc_ref[...] = (jnp.concatenate([alpha, alpha], axis=1) * acc_ref[...]
                        + jnp.dot(p.astype(BF16), v2, preferred_element_type=F32))
        m_ref[...] = m_new

    per = tq // tk
    scores(0, s_refs[0])

    def body(g, carry):
        for d in range(per):
            j = g * per + d
            scores(j + 1, s_refs[(d + 1) % 2])
            consume(j, s_refs[d % 2], None)
        return carry

    lax.fori_loop(0, i, body, 0)
    for d in range(per):
        j = i * per + d
        if d + 1 < per:
            scores(j + 1, s_refs[(d + 1) % 2])
        consume(j, s_refs[d % 2], d * tk)

    lp = lam_ref[...]
    lam = (jnp.exp(jnp.sum(lp[0:1] * lp[1:2], axis=-1, keepdims=True))
           - jnp.exp(jnp.sum(lp[2:3] * lp[3:4], axis=-1, keepdims=True)) + lambda_init)
    o = (acc_ref[0:tq, 0:LANES] / acc_ref[0:tq, LANES:2 * LANES]
         - lam * (acc_ref[tq:2 * tq, 0:LANES] / acc_ref[tq:2 * tq, LANES:2 * LANES]))
    o = o * lax.rsqrt(jnp.mean(o * o, axis=-1, keepdims=True) + DA_EPS) * sub_ref[...]
    o_ref[...] = (o * (1.0 - lambda_init)).astype(o_ref.dtype)


def _attn(p, lam_rows, subln, *, batch, seq, lambda_init, tq=512, tk=256):
    t = batch * seq
    tq = min(tq, seq)
    assert tq & (tq - 1) == 0 and seq % tq == 0 and tq % (2 * tk) == 0
    nq = seq // tq
    return pl.pallas_call(
        functools.partial(_attn_kernel, tq=tq, tk=tk, lambda_init=lambda_init),
        grid=(batch, DA_HEADS, nq),
        in_specs=[
            pl.BlockSpec((tq, LANES), lambda b, h, i: (b * nq + i, P_Q // LANES + h)),
            pl.BlockSpec((seq, LANES), lambda b, h, i: (b, P_K // LANES + h)),
            pl.BlockSpec((seq, LANES), lambda b, h, i: (b, P_V // LANES + h)),
            _resident(lam_rows.shape), _resident(subln.shape),
        ],
        out_specs=pl.BlockSpec((tq, LANES), lambda b, h, i: (b * nq + i, h)),
        out_shape=jax.ShapeDtypeStruct((t, D_MODEL), BF16),
        scratch_shapes=[
            pltpu.VMEM((2 * tq, LANES), BF16), pltpu.VMEM((2 * tq, LANES), F32),
            pltpu.VMEM((2 * tq, 2 * LANES), F32),
            pltpu.VMEM((2 * tq, tk), F32), pltpu.VMEM((2 * tq, tk), F32),
        ],
        compiler_params=_params("parallel", "parallel", "arbitrary"),
    )(p, p, p, lam_rows, subln)


def _conv_kernel(pw_ref, w_ref, b_ref, g_ref, beta_ref, o_ref, win_ref, *, tm, rows):
    i = pl.program_id(1)
    halo = 4 * SUBLANES

    @pl.when(i == 0)
    def _():
        win_ref[0:halo, :] = jnp.zeros((halo, D_MODEL), F32)

    @pl.when(i != 0)
    def _():
        win_ref[0:halo, :] = win_ref[tm:tm + halo, :]

    pa = pw_ref[:, :D_MODEL].astype(F32)
    pg = pw_ref[:, D_MODEL:].astype(F32)
    win_ref[halo:halo + tm, :] = pa * _sigmoid(pg)
    for r0 in range(0, tm, rows):
        acc = jnp.broadcast_to(b_ref[...], (rows, D_MODEL))
        for j in range(CONV_WIDTH):
            off = halo - (CONV_WIDTH - 1) + j + r0
            acc = acc + w_ref[j:j + 1, :] * win_ref[off:off + rows, :]
        mu = jnp.mean(acc, axis=-1, keepdims=True)
        d = acc - mu
        var = jnp.mean(d * d, axis=-1, keepdims=True)
        y = d * lax.rsqrt(var + LN_EPS) * g_ref[...] + beta_ref[...]
        o_ref[r0:r0 + rows, :] = _silu(y).astype(o_ref.dtype)


def _conv(p, w, b, g, beta, *, batch, seq, tm=256, rows=32):
    t = batch * seq
    nt = seq // tm
    return pl.pallas_call(
        functools.partial(_conv_kernel, tm=tm, rows=rows),
        grid=(batch, nt),
        in_specs=[
            pl.BlockSpec((tm, 2 * D_MODEL), lambda bb, i: (bb * nt + i, P_PW // (2 * D_MODEL))),
            _resident(w.shape), _resident(b.shape), _resident(g.shape), _resident(beta.shape),
        ],
        out_specs=pl.BlockSpec((tm, D_MODEL), lambda bb, i: (bb * nt + i, 0)),
        out_shape=jax.ShapeDtypeStruct((t, D_MODEL), BF16),
        scratch_shapes=[pltpu.VMEM((tm + 4 * SUBLANES, D_MODEL), F32)],
        compiler_params=_params("parallel", "arbitrary"),
    )(p, w, b, g, beta)


def _merge_kernel(x_ref, ua_ref, ub_ref, uc_ref, gt_ref, bg_ref, wa_ref, wb_ref, wc_ref, wo_ref, o_ref):
    d = D_MODEL
    merged = None
    for n, (u_ref, w_ref) in enumerate(((ua_ref, wa_ref), (ub_ref, wb_ref), (uc_ref, wc_ref))):
        y = jnp.dot(u_ref[...], w_ref[...], preferred_element_type=F32)
        gate = _sigmoid(gt_ref[:, n * d:(n + 1) * d].astype(F32) + bg_ref[:, n * d:(n + 1) * d])
        merged = gate * y if merged is None else merged + gate * y
    o_ref[...] = x_ref[...] + jnp.dot(merged.astype(BF16), wo_ref[...], preferred_element_type=F32)


def _merge(x, ua, ub, uc, p, bg, wa, wb, wc, wo, *, tm=512):
    t, d = x.shape
    row = lambda i: (i, 0)
    act = pl.BlockSpec((tm, d), row)
    return pl.pallas_call(
        _merge_kernel,
        grid=(t // tm,),
        in_specs=[act, act, act, act, pl.BlockSpec((tm, 3 * d), lambda i: (i, P_GATES // (3 * d))),
                  _resident(bg.shape), _resident(wa.shape), _resident(wb.shape), _resident(wc.shape),
                  _resident(wo.shape)],
        out_specs=act,
        out_shape=jax.ShapeDtypeStruct((t, d), F32),
        compiler_params=_params("parallel"),
    )(x, ua, ub, uc, p, bg, wa, wb, wc, wo)


def _split_w_in(w):
    pts = [0]
    for s in IN_SIZES:
        pts.append(pts[-1] + s)
    gates, z, xbc, dt, q, k, v, pw = (w[:, pts[n]:pts[n + 1]] for n in range(len(IN_SIZES)))
    q = q * (DA_HEAD_DIM ** -0.5 * math.log2(math.e))
    w_main = jnp.concatenate([gates, z, pw, q, k, v, xbc], axis=1).astype(BF16)
    w_dt = jnp.pad(dt, ((0, 0), (0, LANES - SSD_HEADS))).astype(BF16)
    return w_main, w_dt


def _pad_lanes(v):
    return jnp.pad(v, (0, LANES - v.shape[0]))[None, :]


def kernel(x, ffn1_norm, ffn1_w13, ffn1_w2, mix_norm, w_in, b_gate, ssd_conv_w, ssd_conv_b, ssd_dt_bias,
           ssd_a_log, ssd_d, ssd_norm, ssd_wo, da_lq1, da_lk1, da_lq2, da_lk2, da_subln, da_wo, cv_dw_w,
           cv_dw_b, cv_ln_g, cv_ln_b, cv_wo, w_out, ffn2_norm, ffn2_w13, ffn2_w2, final_norm):
    batch, seq, d = x.shape
    depth = ffn1_norm.shape[0]
    xf = x.reshape(batch * seq, d)
    head_of_channel = jnp.arange(D_MODEL, dtype=jnp.int32) // SSD_HEAD_DIM
    expand = (jnp.arange(LANES, dtype=jnp.int32)[:, None] == head_of_channel[None, :]).astype(F32)
    for l in range(depth):
        lambda_init = 0.8 - 0.6 * math.exp(-0.3 * l)
        w_main, w_dt = _split_w_in(w_in[l])
        xf, h, dt_raw = _ffn(xf, ffn1_norm[l][None, :], ffn1_w13[l].astype(BF16), ffn1_w2[l].astype(BF16),
                             mode="mix", extra=(mix_norm[l][None, :], w_dt))
        p = _proj(h, w_main)
        ua = _ssd(p, dt_raw, ssd_conv_w[l], ssd_conv_b[l][None, :], _pad_lanes(ssd_dt_bias[l]),
                  _pad_lanes(-jnp.exp(ssd_a_log[l])), jnp.repeat(ssd_d[l], SSD_HEAD_DIM)[None, :],
                  ssd_norm[l][None, :], expand, batch=batch, seq=seq)
        lam_rows = jnp.pad(jnp.stack([da_lq1[l], da_lk1[l], da_lq2[l], da_lk2[l]]),
                           ((0, SUBLANES - 4), (0, LANES - DA_HEAD_DIM)))
        ub = _attn(p, lam_rows, da_subln[l][None, :], batch=batch, seq=seq, lambda_init=lambda_init)
        uc = _conv(p, cv_dw_w[l], cv_dw_b[l][None, :], cv_ln_g[l][None, :], cv_ln_b[l][None, :],
                   batch=batch, seq=seq)
        xf = _merge(xf, ua, ub, uc, p, b_gate[l][None, :], ssd_wo[l].astype(BF16), da_wo[l].astype(BF16),
                    cv_wo[l].astype(BF16), w_out[l].astype(BF16))
        if l == depth - 1:
            xf = _ffn(xf, ffn2_norm[l][None, :], ffn2_w13[l].astype(BF16), ffn2_w2[l].astype(BF16),
                      mode="final", extra=(final_norm[None, :],))
        else:
            xf = _ffn(xf, ffn2_norm[l][None, :], ffn2_w13[l].astype(BF16), ffn2_w2[l].astype(BF16),
                      mode="plain")
    return xf.reshape(batch, seq, d)
```

```python
import functools
import math

import jax
import jax.numpy as jnp
from jax import lax
from jax.experimental import pallas as pl
from jax.experimental.pallas import tpu as pltpu

F32 = jnp.float32
BF16 = jnp.bfloat16

D_MODEL = 1024
D_FF = 2816
EPS = 1e-6
SSD_HEADS = 16
SSD_HEAD_DIM = 64
SSD_GROUPS = 2
SSD_STATE = 128
SSD_CONV = 4
SSD_CHUNK = 128
SSD_XBC = 1536
DA_HEADS = 8
DA_HEAD_DIM = 64
DA_EPS = 1e-5
LN_EPS = 1e-5
CONV_WIDTH = 31
IN_SIZES = (3 * D_MODEL, D_MODEL, SSD_XBC, SSD_HEADS, D_MODEL, D_MODEL, D_MODEL, 2 * D_MODEL)

LANES = 128
SUBLANES = 8
VMEM_LIMIT_BYTES = 56 * 1024 * 1024

P_GATES = 0
P_Z = 3072
P_PW = 4096
P_Q = 6144
P_K = 7168
P_V = 8192
P_XBC = 9216
P_COLS = 10752


def _params(*sem):
    return pltpu.CompilerParams(dimension_semantics=sem, vmem_limit_bytes=VMEM_LIMIT_BYTES)


def _resident(shape):
    nd = len(shape)
    return pl.BlockSpec(shape, lambda *_: (0,) * nd, pipeline_mode=pl.Buffered(1))


def _sigmoid(x):
    return 1.0 / (1.0 + jnp.exp(-x))


def _silu(x):
    return x * _sigmoid(x)


def _ffn_kernel(*refs, f_chunks, mode):
    if mode == "mix":
        x_ref, g_ref, w13_ref, w2_ref, gm_ref, wdt_ref, xo_ref, h_ref, dt_ref = refs
    elif mode == "final":
        x_ref, g_ref, w13_ref, w2_ref, gf_ref, xo_ref = refs
    else:
        x_ref, g_ref, w13_ref, w2_ref, xo_ref = refs
    x = x_ref[...]
    h = (x * lax.rsqrt(jnp.mean(x * x, axis=-1, keepdims=True) + EPS) * g_ref[...]).astype(BF16)
    acc = jnp.zeros(x.shape, F32)
    lo = 0
    for width in f_chunks:
        gate = jnp.dot(h, w13_ref[:, lo:lo + width], preferred_element_type=F32)
        up = jnp.dot(h, w13_ref[:, D_FF + lo:D_FF + lo + width], preferred_element_type=F32)
        act = (_silu(gate) * up).astype(BF16)
        acc = acc + jnp.dot(act, w2_ref[lo:lo + width, :], preferred_element_type=F32)
        lo += width
    xn = x + 0.5 * acc
    if mode == "final":
        xo_ref[...] = xn * lax.rsqrt(jnp.mean(xn * xn, axis=-1, keepdims=True) + EPS) * gf_ref[...]
        return
    xo_ref[...] = xn
    if mode == "mix":
        hm = (xn * lax.rsqrt(jnp.mean(xn * xn, axis=-1, keepdims=True) + EPS) * gm_ref[...]).astype(BF16)
        h_ref[...] = hm
        dt_ref[...] = jnp.dot(hm, wdt_ref[...], preferred_element_type=F32)


MXU_TILE = 256
FFN_CHUNKS = (6 * MXU_TILE, D_FF - 6 * MXU_TILE)


def _ffn(x, g, w13, w2, *, mode, extra=(), tm=512, f_chunks=FFN_CHUNKS):
    t, d = x.shape
    row = lambda i: (i, 0)
    in_specs = [pl.BlockSpec((tm, d), row), _resident((1, d)), _resident(w13.shape), _resident(w2.shape)]
    out_shape = [jax.ShapeDtypeStruct((t, d), F32)]
    out_specs = [pl.BlockSpec((tm, d), row)]
    if mode == "mix":
        gm, wdt = extra
        in_specs += [_resident((1, d)), _resident(wdt.shape)]
        out_shape += [jax.ShapeDtypeStruct((t, d), BF16), jax.ShapeDtypeStruct((t, LANES), F32)]
        out_specs += [pl.BlockSpec((tm, d), row), pl.BlockSpec((tm, LANES), row)]
    elif mode == "final":
        in_specs += [_resident((1, d))]
    out = pl.pallas_call(
        functools.partial(_ffn_kernel, f_chunks=f_chunks, mode=mode),
        grid=(t // tm,),
        in_specs=in_specs,
        out_specs=out_specs,
        out_shape=out_shape,
        compiler_params=_params("parallel"),
    )(x, g, w13, w2, *extra)
    return out if mode == "mix" else out[0]


def _proj_kernel(h_ref, w_ref, o_ref):
    o_ref[...] = jnp.dot(h_ref[...], w_ref[...], preferred_element_type=F32).astype(o_ref.dtype)


def _proj(h, w, *, tm=1024, tn=1792):
    t, d = h.shape
    n = w.shape[1]
    return pl.pallas_call(
        _proj_kernel,
        grid=(n // tn, t // tm),
        in_specs=[pl.BlockSpec((tm, d), lambda j, i: (i, 0)), pl.BlockSpec((d, tn), lambda j, i: (0, j))],
        out_specs=pl.BlockSpec((tm, tn), lambda j, i: (i, j)),
        out_shape=jax.ShapeDtypeStruct((t, n), BF16),
        compiler_params=_params("parallel", "parallel"),
    )(h, w)


def _ssd_kernel(xbc_ref, z_ref, dt_ref, cw_ref, cb_ref, dtb_ref, aneg_ref, dexp_ref, nrm_ref, expand_ref,
                o_ref, xwin_ref, state_ref):
    c = pl.program_id(1)
    L = SSD_CHUNK
    tail = SUBLANES

    @pl.when(c == 0)
    def _():
        xwin_ref[0:tail, :] = jnp.zeros((tail, SSD_XBC), F32)
        state_ref[...] = jnp.zeros(state_ref.shape, F32)

    @pl.when(c != 0)
    def _():
        xwin_ref[0:tail, :] = xwin_ref[L:L + tail, :]

    xwin_ref[tail:tail + L, :] = xbc_ref[...].astype(F32)
    win = xwin_ref[0:tail + L, :]
    conv = cb_ref[...] + cw_ref[SSD_CONV - 1:SSD_CONV, :] * win[tail:tail + L]
    for j in range(SSD_CONV - 1):
        shifted = pltpu.roll(win, SSD_CONV - 1 - j, axis=0)
        conv = conv + cw_ref[j:j + 1, :] * shifted[tail:tail + L]
    xbc = _silu(conv)
    xs = xbc[:, :D_MODEL]

    dt_in = dt_ref[...] + dtb_ref[...]
    dt = jnp.maximum(dt_in, 0.0) + jnp.log(1.0 + jnp.exp(-jnp.abs(dt_in)))
    a = dt * aneg_ref[...]
    row = lax.broadcasted_iota(jnp.int32, (L, L), 0)
    col = lax.broadcasted_iota(jnp.int32, (L, L), 1)
    causal = col <= row

    def split(v, terms):
        parts = []
        for _ in range(terms):
            piece = v.astype(BF16)
            parts.append(piece)
            v = v - piece.astype(F32)
        return parts

    tril = jnp.where(causal, 1.0, 0.0).astype(BF16)
    a_cum = jnp.dot(jnp.concatenate([tril] * 3, axis=1), jnp.concatenate(split(a, 3), axis=0),
                    preferred_element_type=F32)
    a_cum_t = a_cum.T
    a_last = a_cum[L - 1:L, :]
    exp_a = jnp.exp(a_cum)
    dstate = jnp.exp(a_last - a_cum)
    per_head = jnp.concatenate([jnp.concatenate(split(v, 2), axis=1) for v in (dt, dt * dstate, exp_a)], axis=0)
    per_chan = jnp.dot(per_head, expand_ref[...], preferred_element_type=F32)
    dt_x, dtd_x, expa_x = per_chan[0:L], per_chan[L:2 * L], per_chan[2 * L:3 * L]

    xdt = xs * dt_x
    xdt_state = (xs * dtd_x).astype(BF16)
    lane = lax.broadcasted_iota(jnp.int32, (L, D_MODEL), 1)
    lo_half = (lane & (LANES - 1)) < SSD_HEAD_DIM
    xdt_lo = jnp.where(lo_half, xdt, 0.0).astype(BF16)
    xdt_hi = jnp.where(lo_half, 0.0, xdt).astype(BF16)

    gw = SSD_STATE
    cg = D_MODEL // SSD_GROUPS
    hg = SSD_HEADS // SSD_GROUPS
    y_parts = []
    for g in range(SSD_GROUPS):
        bm = xbc[:, D_MODEL + g * gw:D_MODEL + (g + 1) * gw]
        cm = xbc[:, D_MODEL + SSD_GROUPS * gw + g * gw:D_MODEL + SSD_GROUPS * gw + (g + 1) * gw]
        bm16 = bm.astype(BF16)
        cm16 = cm.astype(BF16)
        cb = lax.dot_general(cm16, bm16, (((1,), (1,)), ((), ())), preferred_element_type=F32)
        for pair in range(hg // 2):
            ms = []
            for k in range(2):
                h = g * hg + pair * 2 + k
                seg = a_cum[:, h:h + 1] - a_cum_t[h:h + 1, :]
                ms.append((cb * jnp.exp(jnp.where(causal, seg, -jnp.inf))).astype(BF16))
            m2 = jnp.concatenate(ms, axis=1)
            c0 = (g * hg + pair * 2) * SSD_HEAD_DIM
            rhs = jnp.concatenate([xdt_lo[:, c0:c0 + LANES], xdt_hi[:, c0:c0 + LANES]], axis=0)
            y_parts.append(jnp.dot(m2, rhs, preferred_element_type=F32))
    y = jnp.concatenate(y_parts, axis=1)

    outs = []
    for g in range(SSD_GROUPS):
        bm = xbc[:, D_MODEL + g * gw:D_MODEL + (g + 1) * gw]
        cm = xbc[:, D_MODEL + SSD_GROUPS * gw + g * gw:D_MODEL + SSD_GROUPS * gw + (g + 1) * gw]
        prev = state_ref[g]
        y_off = jnp.dot(cm.astype(BF16), prev.astype(BF16), preferred_element_type=F32)
        new = jnp.dot(bm.T.astype(BF16), xdt_state[:, g * cg:(g + 1) * cg], preferred_element_type=F32)
        state_ref[g] = prev * expa_x[L - 1:L, g * cg:(g + 1) * cg] + new
        yg = y[:, g * cg:(g + 1) * cg] + y_off * expa_x[:, g * cg:(g + 1) * cg]
        yg = yg + xs[:, g * cg:(g + 1) * cg] * dexp_ref[:, g * cg:(g + 1) * cg]
        u = yg * _silu(z_ref[:, g * cg:(g + 1) * cg].astype(F32))
        u = u * lax.rsqrt(jnp.mean(u * u, axis=-1, keepdims=True) + EPS)
        outs.append(u * nrm_ref[:, g * cg:(g + 1) * cg])
    o_ref[...] = jnp.concatenate(outs, axis=1).astype(o_ref.dtype)


def _ssd(p, dt_raw, cw, cb, dtb, aneg, dexp, nrm, expand, *, batch, seq):
    t = batch * seq
    nc = seq // SSD_CHUNK
    L = SSD_CHUNK
    rowmap = lambda b, c: (b * nc + c, 0)
    return pl.pallas_call(
        _ssd_kernel,
        grid=(batch, nc),
        in_specs=[
            pl.BlockSpec((L, SSD_XBC), lambda b, c: (b * nc + c, P_XBC // SSD_XBC)),
            pl.BlockSpec((L, D_MODEL), lambda b, c: (b * nc + c, P_Z // D_MODEL)),
            pl.BlockSpec((L, LANES), rowmap),
            _resident(cw.shape), _resident(cb.shape), _resident(dtb.shape), _resident(aneg.shape),
            _resident(dexp.shape), _resident(nrm.shape), _resident(expand.shape),
        ],
        out_specs=pl.BlockSpec((L, D_MODEL), rowmap),
        out_shape=jax.ShapeDtypeStruct((t, D_MODEL), BF16),
        scratch_shapes=[
            pltpu.VMEM((L + 2 * SUBLANES, SSD_XBC), F32),
            pltpu.VMEM((SSD_GROUPS, SSD_STATE, D_MODEL // SSD_GROUPS), F32),
        ],
        compiler_params=_params("parallel", "arbitrary"),
    )(p, p, dt_raw, cw, cb, dtb, aneg, dexp, nrm, expand)


def _attn_kernel(q_ref, k_ref, v_ref, lam_ref, sub_ref, o_ref, q2_ref, m_ref, acc_ref, s0_ref, s1_ref, *,
                 tq, tk, lambda_init):
    i = pl.program_id(2)
    per = tq // tk
    first = lax.broadcasted_iota(jnp.int32, (tk, LANES), 1) < DA_HEAD_DIM
    for b in range(per):
        q = q_ref[b * tk:(b + 1) * tk, :]
        zero = jnp.zeros_like(q)
        q2_ref[2 * b * tk:(2 * b + 1) * tk, :] = jnp.where(first, q, zero)
        q2_ref[(2 * b + 1) * tk:(2 * b + 2) * tk, :] = jnp.where(first, zero, q)
    m_ref[...] = jnp.full(m_ref.shape, -jnp.inf, F32)
    acc_ref[...] = jnp.zeros(acc_ref.shape, F32)
    ones = jnp.ones((tk, LANES), BF16)
    nt = (((1,), (1,)), ((), ()))

    s_refs = (s0_ref, s1_ref)

    def scores(j, dst_ref, lo):
        start = pl.multiple_of(j * tk, tk)
        dst_ref[lo:2 * tq, :] = lax.dot_general(q2_ref[lo:2 * tq, :], k_ref[pl.ds(start, tk), :], nt,
                                                preferred_element_type=F32)

    def consume(j, src_ref, diag):
        lo = 0 if diag is None else 2 * diag * tk
        start = pl.multiple_of(j * tk, tk)
        v2 = jnp.concatenate([v_ref[pl.ds(start, tk), :], ones], axis=1)
        s = src_ref[lo:2 * tq, :]
        if diag is not None:
            top = s[0:2 * tk]
            r = lax.broadcasted_iota(jnp.int32, top.shape, 0) & (tk - 1)
            cc = lax.broadcasted_iota(jnp.int32, top.shape, 1)
            top = jnp.where(cc <= r, top, -jnp.inf)
            s = top if diag == per - 1 else jnp.concatenate([top, s[2 * tk:]], axis=0)
        m_old = m_ref[lo:2 * tq, :]
        m_new = jnp.maximum(m_old, jnp.max(s, axis=-1, keepdims=True))
        alpha = jnp.exp2(m_old - m_new)
        p = jnp.exp2(s - jnp.concatenate([m_new] * (tk // LANES), axis=1))
        acc_ref[lo:2 * tq, :] = (jnp.concatenate([alpha, alpha], axis=1) * acc_ref[lo:2 * tq, :]
                                 + jnp.dot(p.astype(BF16), v2, preferred_element_type=F32))
        m_ref[lo:2 * tq, :] = m_new

    scores(0, s_refs[0], 0)

    def body(g, carry):
        for d in range(per):
            j = g * per + d
            scores(j + 1, s_refs[(d + 1) % 2], 0)
            consume(j, s_refs[d % 2], None)
        return carry

    lax.fori_loop(0, i, body, 0)
    for d in range(per):
        j = i * per + d
        if d + 1 < per:
            scores(j + 1, s_refs[(d + 1) % 2], 2 * (d + 1) * tk)
        consume(j, s_refs[d % 2], d)

    lp = lam_ref[...]
    lam = (jnp.exp(jnp.sum(lp[0:1] * lp[1:2], axis=-1, keepdims=True))
           - jnp.exp(jnp.sum(lp[2:3] * lp[3:4], axis=-1, keepdims=True)) + lambda_init)
    for b in range(per):
        r0, r1 = 2 * b * tk, (2 * b + 1) * tk
        o = (acc_ref[r0:r0 + tk, 0:LANES] / acc_ref[r0:r0 + tk, LANES:2 * LANES]
             - lam * (acc_ref[r1:r1 + tk, 0:LANES] / acc_ref[r1:r1 + tk, LANES:2 * LANES]))
        o = o * lax.rsqrt(jnp.mean(o * o, axis=-1, keepdims=True) + DA_EPS) * sub_ref[...]
        o_ref[b * tk:(b + 1) * tk, :] = (o * (1.0 - lambda_init)).astype(o_ref.dtype)


def _attn(p, lam_rows, subln, *, batch, seq, lambda_init, tq=1024, tk=512):
    t = batch * seq
    tq = min(tq, seq)
    assert tk & (tk - 1) == 0 and seq % tq == 0 and tq % (2 * tk) == 0
    nq = seq // tq
    return pl.pallas_call(
        functools.partial(_attn_kernel, tq=tq, tk=tk, lambda_init=lambda_init),
        grid=(batch, DA_HEADS, nq),
        in_specs=[
            pl.BlockSpec((tq, LANES), lambda b, h, i: (b * nq + i, P_Q // LANES + h)),
            pl.BlockSpec((seq, LANES), lambda b, h, i: (b, P_K // LANES + h)),
            pl.BlockSpec((seq, LANES), lambda b, h, i: (b, P_V // LANES + h)),
            _resident(lam_rows.shape), _resident(subln.shape),
        ],
        out_specs=pl.BlockSpec((tq, LANES), lambda b, h, i: (b * nq + i, h)),
        out_shape=jax.ShapeDtypeStruct((t, D_MODEL), BF16),
        scratch_shapes=[
            pltpu.VMEM((2 * tq, LANES), BF16), pltpu.VMEM((2 * tq, LANES), F32),
            pltpu.VMEM((2 * tq, 2 * LANES), F32),
            pltpu.VMEM((2 * tq, tk), F32), pltpu.VMEM((2 * tq, tk), F32),
        ],
        compiler_params=_params("parallel", "parallel", "arbitrary"),
    )(p, p, p, lam_rows, subln)


def _conv_kernel(pw_ref, w_ref, b_ref, g_ref, beta_ref, o_ref, win_ref, y_ref, *, tm, rows, lanes):
    i = pl.program_id(1)
    halo = 4 * SUBLANES

    @pl.when(i == 0)
    def _():
        win_ref[0:halo, :] = jnp.zeros((halo, D_MODEL), F32)

    @pl.when(i != 0)
    def _():
        win_ref[0:halo, :] = win_ref[tm:tm + halo, :]

    pa = pw_ref[:, :D_MODEL].astype(F32)
    pg = pw_ref[:, D_MODEL:].astype(F32)
    win_ref[halo:halo + tm, :] = pa * _sigmoid(pg)
    first = halo - (CONV_WIDTH - 1)
    for r0 in range(0, tm, rows):
        for c0 in range(0, D_MODEL, lanes):
            cols = slice(c0, c0 + lanes)
            acc = jnp.broadcast_to(b_ref[:, cols], (rows, lanes))
            for r in range(SUBLANES):
                n = rows if r == 0 else rows + SUBLANES
                z = None
                for o in range(r, halo + 1, SUBLANES):
                    if o < first:
                        continue
                    j = o - first
                    term = w_ref[j:j + 1, cols] * win_ref[r0 + o - r:r0 + o - r + n, cols]
                    z = term if z is None else z + term
                acc = acc + z[r:r + rows]
            y_ref[r0:r0 + rows, cols] = acc
    for r0 in range(0, tm, rows):
        acc = y_ref[r0:r0 + rows, :]
        mu = jnp.mean(acc, axis=-1, keepdims=True)
        d = acc - mu
        var = jnp.mean(d * d, axis=-1, keepdims=True)
        y = d * lax.rsqrt(var + LN_EPS) * g_ref[...] + beta_ref[...]
        o_ref[r0:r0 + rows, :] = _silu(y).astype(o_ref.dtype)


def _conv(p, w, b, g, beta, *, batch, seq, tm=256, rows=128, lanes=128):
    t = batch * seq
    nt = seq // tm
    return pl.pallas_call(
        functools.partial(_conv_kernel, tm=tm, rows=rows, lanes=lanes),
        grid=(batch, nt),
        in_specs=[
            pl.BlockSpec((tm, 2 * D_MODEL), lambda bb, i: (bb * nt + i, P_PW // (2 * D_MODEL))),
            _resident(w.shape), _resident(b.shape), _resident(g.shape), _resident(beta.shape),
        ],
        out_specs=pl.BlockSpec((tm, D_MODEL), lambda bb, i: (bb * nt + i, 0)),
        out_shape=jax.ShapeDtypeStruct((t, D_MODEL), BF16),
        scratch_shapes=[pltpu.VMEM((tm + 4 * SUBLANES, D_MODEL), F32), pltpu.VMEM((tm, D_MODEL), F32)],
        compiler_params=_params("parallel", "arbitrary"),
    )(p, w, b, g, beta)


def _merge_kernel(x_ref, ua_ref, ub_ref, uc_ref, gt_ref, bg_ref, wa_ref, wb_ref, wc_ref, wo_ref, o_ref):
    d = D_MODEL
    merged = None
    for n, (u_ref, w_ref) in enumerate(((ua_ref, wa_ref), (ub_ref, wb_ref), (uc_ref, wc_ref))):
        y = jnp.dot(u_ref[...], w_ref[...], preferred_element_type=F32)
        gate = _sigmoid(gt_ref[:, n * d:(n + 1) * d].astype(F32) + bg_ref[:, n * d:(n + 1) * d])
        merged = gate * y if merged is None else merged + gate * y
    o_ref[...] = x_ref[...] + jnp.dot(merged.astype(BF16), wo_ref[...], preferred_element_type=F32)


def _merge(x, ua, ub, uc, p, bg, wa, wb, wc, wo, *, tm=512):
    t, d = x.shape
    row = lambda i: (i, 0)
    act = pl.BlockSpec((tm, d), row)
    return pl.pallas_call(
        _merge_kernel,
        grid=(t // tm,),
        in_specs=[act, act, act, act, pl.BlockSpec((tm, 3 * d), lambda i: (i, P_GATES // (3 * d))),
                  _resident(bg.shape), _resident(wa.shape), _resident(wb.shape), _resident(wc.shape),
                  _resident(wo.shape)],
        out_specs=act,
        out_shape=jax.ShapeDtypeStruct((t, d), F32),
        compiler_params=_params("parallel"),
    )(x, ua, ub, uc, p, bg, wa, wb, wc, wo)


def _split_w_in(w):
    pts = [0]
    for s in IN_SIZES:
        pts.append(pts[-1] + s)
    gates, z, xbc, dt, q, k, v, pw = (w[:, pts[n]:pts[n + 1]] for n in range(len(IN_SIZES)))
    q = q * (DA_HEAD_DIM ** -0.5 * math.log2(math.e))
    w_main = jnp.concatenate([gates, z, pw, q, k, v, xbc], axis=1).astype(BF16)
    w_dt = jnp.pad(dt, ((0, 0), (0, LANES - SSD_HEADS))).astype(BF16)
    return w_main, w_dt


def _pad_lanes(v):
    return jnp.pad(v, (0, LANES - v.shape[0]))[None, :]


def kernel(x, ffn1_norm, ffn1_w13, ffn1_w2, mix_norm, w_in, b_gate, ssd_conv_w, ssd_conv_b, ssd_dt_bias,
           ssd_a_log, ssd_d, ssd_norm, ssd_wo, da_lq1, da_lk1, da_lq2, da_lk2, da_subln, da_wo, cv_dw_w,
           cv_dw_b, cv_ln_g, cv_ln_b, cv_wo, w_out, ffn2_norm, ffn2_w13, ffn2_w2, final_norm):
    batch, seq, d = x.shape
    depth = ffn1_norm.shape[0]
    xf = x.reshape(batch * seq, d)
    head_of_channel = jnp.arange(D_MODEL, dtype=jnp.int32) // SSD_HEAD_DIM
    expand = (jnp.arange(LANES, dtype=jnp.int32)[:, None] == head_of_channel[None, :]).astype(BF16)
    expand = jnp.concatenate([expand, expand], axis=0)
    for l in range(depth):
        lambda_init = 0.8 - 0.6 * math.exp(-0.3 * l)
        w_main, w_dt = _split_w_in(w_in[l])
        xf, h, dt_raw = _ffn(xf, ffn1_norm[l][None, :], ffn1_w13[l].astype(BF16), ffn1_w2[l].astype(BF16),
                             mode="mix", extra=(mix_norm[l][None, :], w_dt))
        p = _proj(h, w_main)
        ua = _ssd(p, dt_raw, ssd_conv_w[l], ssd_conv_b[l][None, :], _pad_lanes(ssd_dt_bias[l]),
                  _pad_lanes(-jnp.exp(ssd_a_log[l])), jnp.repeat(ssd_d[l], SSD_HEAD_DIM)[None, :],
                  ssd_norm[l][None, :], expand, batch=batch, seq=seq)
        lam_rows = jnp.pad(jnp.stack([da_lq1[l], da_lk1[l], da_lq2[l], da_lk2[l]]),
                           ((0, SUBLANES - 4), (0, LANES - DA_HEAD_DIM)))
        ub = _attn(p, lam_rows, da_subln[l][None, :], batch=batch, seq=seq, lambda_init=lambda_init)
        uc = _conv(p, cv_dw_w[l], cv_dw_b[l][None, :], cv_ln_g[l][None, :], cv_ln_b[l][None, :],
                   batch=batch, seq=seq)
        xf = _merge(xf, ua, ub, uc, p, b_gate[l][None, :], ssd_wo[l].astype(BF16), da_wo[l].astype(BF16),
                    cv_wo[l].astype(BF16), w_out[l].astype(BF16))
        if l == depth - 1:
            xf = _ffn(xf, ffn2_norm[l][None, :], ffn2_w13[l].astype(BF16), ffn2_w2[l].astype(BF16),
                      mode="final", extra=(final_norm[None, :],))
        else:
            xf = _ffn(xf, ffn2_norm[l][None, :], ffn2_w13[l].astype(BF16), ffn2_w2[l].astype(BF16),
                      mode="plain")
    return xf.reshape(batch, seq, d)
```

```python
import functools
import math

import jax
import jax.numpy as jnp
from jax import lax
from jax.experimental import pallas as pl
from jax.experimental.pallas import tpu as pltpu

F32 = jnp.float32
BF16 = jnp.bfloat16

D_MODEL = 1024
D_FF = 2816
EPS = 1e-6
SSD_HEADS = 16
SSD_HEAD_DIM = 64
SSD_GROUPS = 2
SSD_STATE = 128
SSD_CONV = 4
SSD_CHUNK = 128
SSD_XBC = 1536
DA_HEADS = 8
DA_HEAD_DIM = 64
DA_EPS = 1e-5
LN_EPS = 1e-5
CONV_WIDTH = 31
IN_SIZES = (3 * D_MODEL, D_MODEL, SSD_XBC, SSD_HEADS, D_MODEL, D_MODEL, D_MODEL, 2 * D_MODEL)

LANES = 128
SUBLANES = 8
VMEM_LIMIT_BYTES = 56 * 1024 * 1024

P_GATES = 0
P_Z = 3072
P_PW = 4096
P_Q = 6144
P_K = 7168
P_V = 8192
P_XBC = 9216
P_COLS = 10752


def _params(*sem):
    return pltpu.CompilerParams(dimension_semantics=sem, vmem_limit_bytes=VMEM_LIMIT_BYTES)


def _resident(shape):
    nd = len(shape)
    return pl.BlockSpec(shape, lambda *_: (0,) * nd, pipeline_mode=pl.Buffered(1))


def _sigmoid(x):
    return 1.0 / (1.0 + jnp.exp(-x))


def _silu(x):
    return x * _sigmoid(x)


def _ffn_kernel(*refs, f_chunks, mode):
    if mode == "mix":
        x_ref, g_ref, w13_ref, w2_ref, gm_ref, wdt_ref, xo_ref, h_ref, dt_ref = refs
    elif mode == "final":
        x_ref, g_ref, w13_ref, w2_ref, gf_ref, xo_ref = refs
    else:
        x_ref, g_ref, w13_ref, w2_ref, xo_ref = refs
    x = x_ref[...]
    h = (x * lax.rsqrt(jnp.mean(x * x, axis=-1, keepdims=True) + EPS) * g_ref[...]).astype(BF16)
    acc = jnp.zeros(x.shape, F32)
    lo = 0
    for width in f_chunks:
        gate = jnp.dot(h, w13_ref[:, lo:lo + width], preferred_element_type=F32)
        up = jnp.dot(h, w13_ref[:, D_FF + lo:D_FF + lo + width], preferred_element_type=F32)
        act = (_silu(gate) * up).astype(BF16)
        acc = acc + jnp.dot(act, w2_ref[lo:lo + width, :], preferred_element_type=F32)
        lo += width
    xn = x + 0.5 * acc
    if mode == "final":
        xo_ref[...] = xn * lax.rsqrt(jnp.mean(xn * xn, axis=-1, keepdims=True) + EPS) * gf_ref[...]
        return
    xo_ref[...] = xn
    if mode == "mix":
        hm = (xn * lax.rsqrt(jnp.mean(xn * xn, axis=-1, keepdims=True) + EPS) * gm_ref[...]).astype(BF16)
        h_ref[...] = hm
        dt_ref[...] = jnp.dot(hm, wdt_ref[...], preferred_element_type=F32)


MXU_TILE = 256
FFN_CHUNKS = (6 * MXU_TILE, D_FF - 6 * MXU_TILE)


def _ffn(x, g, w13, w2, *, mode, extra=(), tm=256, f_chunks=FFN_CHUNKS):
    t, d = x.shape
    row = lambda i: (i, 0)
    in_specs = [pl.BlockSpec((tm, d), row), _resident((1, d)), _resident(w13.shape), _resident(w2.shape)]
    out_shape = [jax.ShapeDtypeStruct((t, d), F32)]
    out_specs = [pl.BlockSpec((tm, d), row)]
    if mode == "mix":
        gm, wdt = extra
        in_specs += [_resident((1, d)), _resident(wdt.shape)]
        out_shape += [jax.ShapeDtypeStruct((t, d), BF16), jax.ShapeDtypeStruct((t, LANES), F32)]
        out_specs += [pl.BlockSpec((tm, d), row), pl.BlockSpec((tm, LANES), row)]
    elif mode == "final":
        in_specs += [_resident((1, d))]
    out = pl.pallas_call(
        functools.partial(_ffn_kernel, f_chunks=f_chunks, mode=mode),
        grid=(t // tm,),
        in_specs=in_specs,
        out_specs=out_specs,
        out_shape=out_shape,
        compiler_params=_params("parallel"),
    )(x, g, w13, w2, *extra)
    return out if mode == "mix" else out[0]


def _proj_kernel(h_ref, w_ref, o_ref):
    o_ref[...] = jnp.dot(h_ref[...], w_ref[...], preferred_element_type=F32).astype(o_ref.dtype)


def _proj(h, w, *, tm=1024, tn=1792):
    t, d = h.shape
    n = w.shape[1]
    return pl.pallas_call(
        _proj_kernel,
        grid=(n // tn, t // tm),
        in_specs=[pl.BlockSpec((tm, d), lambda j, i: (i, 0)), pl.BlockSpec((d, tn), lambda j, i: (0, j))],
        out_specs=pl.BlockSpec((tm, tn), lambda j, i: (i, j)),
        out_shape=jax.ShapeDtypeStruct((t, n), BF16),
        compiler_params=_params("parallel", "parallel"),
    )(h, w)


def _ssd_kernel(xbc_ref, z_ref, dt_ref, cw_ref, cb_ref, dtb_ref, aneg_ref, dexp_ref, nrm_ref, expand_ref,
                o_ref, xwin_ref, state_ref):
    c = pl.program_id(1)
    L = SSD_CHUNK
    tail = SUBLANES

    @pl.when(c == 0)
    def _():
        xwin_ref[0:tail, :] = jnp.zeros((tail, SSD_XBC), F32)
        state_ref[...] = jnp.zeros(state_ref.shape, F32)

    @pl.when(c != 0)
    def _():
        xwin_ref[0:tail, :] = xwin_ref[L:L + tail, :]

    xwin_ref[tail:tail + L, :] = xbc_ref[...].astype(F32)
    win = xwin_ref[0:tail + L, :]
    conv = cb_ref[...] + cw_ref[SSD_CONV - 1:SSD_CONV, :] * win[tail:tail + L]
    for j in range(SSD_CONV - 1):
        shifted = pltpu.roll(win, SSD_CONV - 1 - j, axis=0)
        conv = conv + cw_ref[j:j + 1, :] * shifted[tail:tail + L]
    xbc = _silu(conv)
    xs = xbc[:, :D_MODEL]

    dt_in = dt_ref[...] + dtb_ref[...]
    dt = jnp.maximum(dt_in, 0.0) + jnp.log(1.0 + jnp.exp(-jnp.abs(dt_in)))
    a = dt * aneg_ref[...]
    row = lax.broadcasted_iota(jnp.int32, (L, L), 0)
    col = lax.broadcasted_iota(jnp.int32, (L, L), 1)
    causal = col <= row

    def split(v, terms):
        parts = []
        for _ in range(terms):
            piece = v.astype(BF16)
            parts.append(piece)
            v = v - piece.astype(F32)
        return parts

    tril = jnp.where(causal, 1.0, 0.0).astype(BF16)
    a_cum = jnp.dot(jnp.concatenate([tril] * 3, axis=1), jnp.concatenate(split(a, 3), axis=0),
                    preferred_element_type=F32)
    a_cum_t = a_cum.T
    a_last = a_cum[L - 1:L, :]
    exp_a = jnp.exp(a_cum)
    dstate = jnp.exp(a_last - a_cum)
    per_head = jnp.concatenate([jnp.concatenate(split(v, 2), axis=1) for v in (dt, dt * dstate, exp_a)], axis=0)
    per_chan = jnp.dot(per_head, expand_ref[...], preferred_element_type=F32)
    dt_x, dtd_x, expa_x = per_chan[0:L], per_chan[L:2 * L], per_chan[2 * L:3 * L]

    xdt = xs * dt_x
    xdt_state = (xs * dtd_x).astype(BF16)
    lane = lax.broadcasted_iota(jnp.int32, (L, D_MODEL), 1)
    lo_half = (lane & (LANES - 1)) < SSD_HEAD_DIM
    xdt_lo = jnp.where(lo_half, xdt, 0.0).astype(BF16)
    xdt_hi = jnp.where(lo_half, 0.0, xdt).astype(BF16)

    gw = SSD_STATE
    cg = D_MODEL // SSD_GROUPS
    hg = SSD_HEADS // SSD_GROUPS
    y_parts = []
    for g in range(SSD_GROUPS):
        bm = xbc[:, D_MODEL + g * gw:D_MODEL + (g + 1) * gw]
        cm = xbc[:, D_MODEL + SSD_GROUPS * gw + g * gw:D_MODEL + SSD_GROUPS * gw + (g + 1) * gw]
        bm16 = bm.astype(BF16)
        cm16 = cm.astype(BF16)
        cb = lax.dot_general(cm16, bm16, (((1,), (1,)), ((), ())), preferred_element_type=F32)
        for pair in range(hg // 2):
            ms = []
            for k in range(2):
                h = g * hg + pair * 2 + k
                seg = a_cum[:, h:h + 1] - a_cum_t[h:h + 1, :]
                ms.append((cb * jnp.exp(jnp.where(causal, seg, -jnp.inf))).astype(BF16))
            m2 = jnp.concatenate(ms, axis=1)
            c0 = (g * hg + pair * 2) * SSD_HEAD_DIM
            rhs = jnp.concatenate([xdt_lo[:, c0:c0 + LANES], xdt_hi[:, c0:c0 + LANES]], axis=0)
            y_parts.append(jnp.dot(m2, rhs, preferred_element_type=F32))
    y = jnp.concatenate(y_parts, axis=1)

    outs = []
    for g in range(SSD_GROUPS):
        bm = xbc[:, D_MODEL + g * gw:D_MODEL + (g + 1) * gw]
        cm = xbc[:, D_MODEL + SSD_GROUPS * gw + g * gw:D_MODEL + SSD_GROUPS * gw + (g + 1) * gw]
        prev = state_ref[g]
        y_off = jnp.dot(cm.astype(BF16), prev.astype(BF16), preferred_element_type=F32)
        new = jnp.dot(bm.T.astype(BF16), xdt_state[:, g * cg:(g + 1) * cg], preferred_element_type=F32)
        state_ref[g] = prev * expa_x[L - 1:L, g * cg:(g + 1) * cg] + new
        yg = y[:, g * cg:(g + 1) * cg] + y_off * expa_x[:, g * cg:(g + 1) * cg]
        yg = yg + xs[:, g * cg:(g + 1) * cg] * dexp_ref[:, g * cg:(g + 1) * cg]
        u = yg * _silu(z_ref[:, g * cg:(g + 1) * cg].astype(F32))
        u = u * lax.rsqrt(jnp.mean(u * u, axis=-1, keepdims=True) + EPS)
        outs.append(u * nrm_ref[:, g * cg:(g + 1) * cg])
    o_ref[...] = jnp.concatenate(outs, axis=1).astype(o_ref.dtype)


def _ssd(p, dt_raw, cw, cb, dtb, aneg, dexp, nrm, expand, *, batch, seq):
    t = batch * seq
    nc = seq // SSD_CHUNK
    L = SSD_CHUNK
    rowmap = lambda b, c: (b * nc + c, 0)
    return pl.pallas_call(
        _ssd_kernel,
        grid=(batch, nc),
        in_specs=[
            pl.BlockSpec((L, SSD_XBC), lambda b, c: (b * nc + c, P_XBC // SSD_XBC)),
            pl.BlockSpec((L, D_MODEL), lambda b, c: (b * nc + c, P_Z // D_MODEL)),
            pl.BlockSpec((L, LANES), rowmap),
            _resident(cw.shape), _resident(cb.shape), _resident(dtb.shape), _resident(aneg.shape),
            _resident(dexp.shape), _resident(nrm.shape), _resident(expand.shape),
        ],
        out_specs=pl.BlockSpec((L, D_MODEL), rowmap),
        out_shape=jax.ShapeDtypeStruct((t, D_MODEL), BF16),
        scratch_shapes=[
            pltpu.VMEM((L + 2 * SUBLANES, SSD_XBC), F32),
            pltpu.VMEM((SSD_GROUPS, SSD_STATE, D_MODEL // SSD_GROUPS), F32),
        ],
        compiler_params=_params("parallel", "arbitrary"),
    )(p, p, dt_raw, cw, cb, dtb, aneg, dexp, nrm, expand)


def _attn_kernel(q_ref, k_ref, v_ref, lam_ref, sub_ref, o_ref, q2_ref, m_ref, acc_ref, s0_ref, s1_ref, *,
                 tq, tk, nq, lambda_init):
    per = tq // tk
    first = lax.broadcasted_iota(jnp.int32, (tk, LANES), 1) < DA_HEAD_DIM

    def stack(t, carry):
        q = q_ref[pl.ds(pl.multiple_of(t * tk, tk), tk), :]
        zero = jnp.zeros_like(q)
        dst = pl.multiple_of(2 * t * tk, tk)
        q2_ref[pl.ds(dst, tk), :] = jnp.where(first, q, zero)
        q2_ref[pl.ds(dst + tk, tk), :] = jnp.where(first, zero, q)
        return carry

    lax.fori_loop(0, nq * per, stack, 0)
    ones = jnp.ones((tk, LANES), BF16)
    nt = (((1,), (1,)), ((), ()))
    s_refs = (s0_ref, s1_ref)

    def scores(qi, j, dst_ref, lo):
        rows = pl.ds(pl.multiple_of(qi * 2 * tq + lo, tk), 2 * tq - lo)
        keys = pl.ds(pl.multiple_of(j * tk, tk), tk)
        dst_ref[lo:2 * tq, :] = lax.dot_general(q2_ref[rows, :], k_ref[keys, :], nt, preferred_element_type=F32)

    def consume(j, src_ref, diag):
        lo = 0 if diag is None else 2 * diag * tk
        start = pl.multiple_of(j * tk, tk)
        v2 = jnp.concatenate([v_ref[pl.ds(start, tk), :], ones], axis=1)
        s = src_ref[lo:2 * tq, :]
        if diag is not None:
            top = s[0:2 * tk]
            r = lax.broadcasted_iota(jnp.int32, top.shape, 0) & (tk - 1)
            cc = lax.broadcasted_iota(jnp.int32, top.shape, 1)
            top = jnp.where(cc <= r, top, -jnp.inf)
            s = top if diag == per - 1 else jnp.concatenate([top, s[2 * tk:]], axis=0)
        m_old = m_ref[lo:2 * tq, :]
        m_new = jnp.maximum(m_old, jnp.max(s, axis=-1, keepdims=True))
        alpha = jnp.exp2(m_old - m_new)
        p = jnp.exp2(s - jnp.concatenate([m_new] * (tk // LANES), axis=1))
        acc_ref[lo:2 * tq, :] = (jnp.concatenate([alpha, alpha], axis=1) * acc_ref[lo:2 * tq, :]
                                 + jnp.dot(p.astype(BF16), v2, preferred_element_type=F32))
        m_ref[lo:2 * tq, :] = m_new

    lp = lam_ref[...]
    lam = (jnp.exp(jnp.sum(lp[0:1] * lp[1:2], axis=-1, keepdims=True))
           - jnp.exp(jnp.sum(lp[2:3] * lp[3:4], axis=-1, keepdims=True)) + lambda_init)

    scores(0, 0, s_refs[0], 0)

    def query_tile(qi, carry):
        m_ref[...] = jnp.full(m_ref.shape, -jnp.inf, F32)
        acc_ref[...] = jnp.zeros(acc_ref.shape, F32)

        def body(g, c):
            for d in range(per):
                j = g * per + d
                scores(qi, j + 1, s_refs[(d + 1) % 2], 0)
                consume(j, s_refs[d % 2], None)
            return c

        lax.fori_loop(0, qi, body, 0)
        for d in range(per):
            j = qi * per + d
            if d + 1 < per:
                scores(qi, j + 1, s_refs[(d + 1) % 2], 2 * (d + 1) * tk)
            else:
                scores(jnp.minimum(qi + 1, nq - 1), 0, s_refs[0], 0)
            consume(j, s_refs[d % 2], d)

        for b in range(per):
            r0, r1 = 2 * b * tk, (2 * b + 1) * tk
            o = (acc_ref[r0:r0 + tk, 0:LANES] / acc_ref[r0:r0 + tk, LANES:2 * LANES]
                 - lam * (acc_ref[r1:r1 + tk, 0:LANES] / acc_ref[r1:r1 + tk, LANES:2 * LANES]))
            o = o * lax.rsqrt(jnp.mean(o * o, axis=-1, keepdims=True) + DA_EPS) * sub_ref[...]
            out_rows = pl.ds(pl.multiple_of(qi * tq + b * tk, tk), tk)
            o_ref[out_rows, :] = (o * (1.0 - lambda_init)).astype(o_ref.dtype)
        return carry

    lax.fori_loop(0, nq, query_tile, 0)


def _attn(p, lam_rows, subln, *, batch, seq, lambda_init, tq=1024, tk=512):
    t = batch * seq
    tq = min(tq, seq)
    assert tk & (tk - 1) == 0 and seq % tq == 0 and tq % (2 * tk) == 0
    nq = seq // tq
    head_block = lambda col: pl.BlockSpec((seq, LANES), lambda b, h: (b, col // LANES + h))
    return pl.pallas_call(
        functools.partial(_attn_kernel, tq=tq, tk=tk, nq=nq, lambda_init=lambda_init),
        grid=(batch, DA_HEADS),
        in_specs=[head_block(P_Q), head_block(P_K), head_block(P_V),
                  _resident(lam_rows.shape), _resident(subln.shape)],
        out_specs=pl.BlockSpec((seq, LANES), lambda b, h: (b, h)),
        out_shape=jax.ShapeDtypeStruct((t, D_MODEL), BF16),
        scratch_shapes=[
            pltpu.VMEM((2 * seq, LANES), BF16), pltpu.VMEM((2 * tq, LANES), F32),
            pltpu.VMEM((2 * tq, 2 * LANES), F32),
            pltpu.VMEM((2 * tq, tk), F32), pltpu.VMEM((2 * tq, tk), F32),
        ],
        compiler_params=_params("parallel", "parallel"),
    )(p, p, p, lam_rows, subln)


def _conv_kernel(pw_ref, w_ref, b_ref, g_ref, beta_ref, o_ref, win_ref, y_ref, *, tm, rows, lanes):
    i = pl.program_id(1)
    halo = 4 * SUBLANES

    @pl.when(i == 0)
    def _():
        win_ref[0:halo, :] = jnp.zeros((halo, D_MODEL), F32)

    @pl.when(i != 0)
    def _():
        win_ref[0:halo, :] = win_ref[tm:tm + halo, :]

    pa = pw_ref[:, :D_MODEL].astype(F32)
    pg = pw_ref[:, D_MODEL:].astype(F32)
    win_ref[halo:halo + tm, :] = pa * _sigmoid(pg)
    first = halo - (CONV_WIDTH - 1)
    for r0 in range(0, tm, rows):
        for c0 in range(0, D_MODEL, lanes):
            cols = slice(c0, c0 + lanes)
            acc = jnp.broadcast_to(b_ref[:, cols], (rows, lanes))
            for r in range(SUBLANES):
                n = rows if r == 0 else rows + SUBLANES
                z = None
                for o in range(r, halo + 1, SUBLANES):
                    if o < first:
                        continue
                    j = o - first
                    term = w_ref[j:j + 1, cols] * win_ref[r0 + o - r:r0 + o - r + n, cols]
                    z = term if z is None else z + term
                acc = acc + z[r:r + rows]
            y_ref[r0:r0 + rows, cols] = acc
    for r0 in range(0, tm, rows):
        acc = y_ref[r0:r0 + rows, :]
        mu = jnp.mean(acc, axis=-1, keepdims=True)
        d = acc - mu
        var = jnp.mean(d * d, axis=-1, keepdims=True)
        y = d * lax.rsqrt(var + LN_EPS) * g_ref[...] + beta_ref[...]
        o_ref[r0:r0 + rows, :] = _silu(y).astype(o_ref.dtype)


def _conv(p, w, b, g, beta, *, batch, seq, tm=256, rows=128, lanes=128):
    t = batch * seq
    nt = seq // tm
    return pl.pallas_call(
        functools.partial(_conv_kernel, tm=tm, rows=rows, lanes=lanes),
        grid=(batch, nt),
        in_specs=[
            pl.BlockSpec((tm, 2 * D_MODEL), lambda bb, i: (bb * nt + i, P_PW // (2 * D_MODEL))),
            _resident(w.shape), _resident(b.shape), _resident(g.shape), _resident(beta.shape),
        ],
        out_specs=pl.BlockSpec((tm, D_MODEL), lambda bb, i: (bb * nt + i, 0)),
        out_shape=jax.ShapeDtypeStruct((t, D_MODEL), BF16),
        scratch_shapes=[pltpu.VMEM((tm + 4 * SUBLANES, D_MODEL), F32), pltpu.VMEM((tm, D_MODEL), F32)],
        compiler_params=_params("parallel", "arbitrary"),
    )(p, w, b, g, beta)


def _merge_kernel(x_ref, ua_ref, ub_ref, uc_ref, gt_ref, bg_ref, wa_ref, wb_ref, wc_ref, wo_ref, o_ref):
    d = D_MODEL
    merged = None
    for n, (u_ref, w_ref) in enumerate(((ua_ref, wa_ref), (ub_ref, wb_ref), (uc_ref, wc_ref))):
        y = jnp.dot(u_ref[...], w_ref[...], preferred_element_type=F32)
        gate = _sigmoid(gt_ref[:, n * d:(n + 1) * d].astype(F32) + bg_ref[:, n * d:(n + 1) * d])
        merged = gate * y if merged is None else merged + gate * y
    o_ref[...] = x_ref[...] + jnp.dot(merged.astype(BF16), wo_ref[...], preferred_element_type=F32)


def _merge(x, ua, ub, uc, p, bg, wa, wb, wc, wo, *, tm=512):
    t, d = x.shape
    row = lambda i: (i, 0)
    act = pl.BlockSpec((tm, d), row)
    return pl.pallas_call(
        _merge_kernel,
        grid=(t // tm,),
        in_specs=[act, act, act, act, pl.BlockSpec((tm, 3 * d), lambda i: (i, P_GATES // (3 * d))),
                  _resident(bg.shape), _resident(wa.shape), _resident(wb.shape), _resident(wc.shape),
                  _resident(wo.shape)],
        out_specs=act,
        out_shape=jax.ShapeDtypeStruct((t, d), F32),
        compiler_params=_params("parallel"),
    )(x, ua, ub, uc, p, bg, wa, wb, wc, wo)


def _split_w_in(w):
    pts = [0]
    for s in IN_SIZES:
        pts.append(pts[-1] + s)
    gates, z, xbc, dt, q, k, v, pw = (w[:, pts[n]:pts[n + 1]] for n in range(len(IN_SIZES)))
    q = q * (DA_HEAD_DIM ** -0.5 * math.log2(math.e))
    w_main = jnp.concatenate([gates, z, pw, q, k, v, xbc], axis=1).astype(BF16)
    w_dt = jnp.pad(dt, ((0, 0), (0, LANES - SSD_HEADS))).astype(BF16)
    return w_main, w_dt


def _pad_lanes(v):
    return jnp.pad(v, (0, LANES - v.shape[0]))[None, :]


def kernel(x, ffn1_norm, ffn1_w13, ffn1_w2, mix_norm, w_in, b_gate, ssd_conv_w, ssd_conv_b, ssd_dt_bias,
           ssd_a_log, ssd_d, ssd_norm, ssd_wo, da_lq1, da_lk1, da_lq2, da_lk2, da_subln, da_wo, cv_dw_w,
           cv_dw_b, cv_ln_g, cv_ln_b, cv_wo, w_out, ffn2_norm, ffn2_w13, ffn2_w2, final_norm):
    batch, seq, d = x.shape
    depth = ffn1_norm.shape[0]
    xf = x.reshape(batch * seq, d)
    head_of_channel = jnp.arange(D_MODEL, dtype=jnp.int32) // SSD_HEAD_DIM
    expand = (jnp.arange(LANES, dtype=jnp.int32)[:, None] == head_of_channel[None, :]).astype(BF16)
    expand = jnp.concatenate([expand, expand], axis=0)
    for l in range(depth):
        lambda_init = 0.8 - 0.6 * math.exp(-0.3 * l)
        w_main, w_dt = _split_w_in(w_in[l])
        xf, h, dt_raw = _ffn(xf, ffn1_norm[l][None, :], ffn1_w13[l].astype(BF16), ffn1_w2[l].astype(BF16),
                             mode="mix", extra=(mix_norm[l][None, :], w_dt))
        p = _proj(h, w_main)
        ua = _ssd(p, dt_raw, ssd_conv_w[l], ssd_conv_b[l][None, :], _pad_lanes(ssd_dt_bias[l]),
                  _pad_lanes(-jnp.exp(ssd_a_log[l])), jnp.repeat(ssd_d[l], SSD_HEAD_DIM)[None, :],
                  ssd_norm[l][None, :], expand, batch=batch, seq=seq)
        lam_rows = jnp.pad(jnp.stack([da_lq1[l], da_lk1[l], da_lq2[l], da_lk2[l]]),
                           ((0, SUBLANES - 4), (0, LANES - DA_HEAD_DIM)))
        ub = _attn(p, lam_rows, da_subln[l][None, :], batch=batch, seq=seq, lambda_init=lambda_init)
        uc = _conv(p, cv_dw_w[l], cv_dw_b[l][None, :], cv_ln_g[l][None, :], cv_ln_b[l][None, :],
                   batch=batch, seq=seq)
        xf = _merge(xf, ua, ub, uc, p, b_gate[l][None, :], ssd_wo[l].astype(BF16), da_wo[l].astype(BF16),
                    cv_wo[l].astype(BF16), w_out[l].astype(BF16))
        if l == depth - 1:
            xf = _ffn(xf, ffn2_norm[l][None, :], ffn2_w13[l].astype(BF16), ffn2_w2[l].astype(BF16),
                      mode="final", extra=(final_norm[None, :],))
        else:
            xf = _ffn(xf, ffn2_norm[l][None, :], ffn2_w13[l].astype(BF16), ffn2_w2[l].astype(BF16),
                      mode="plain")
    return xf.reshape(batch, seq, d)
```

```python
import functools
import math

import jax
import jax.numpy as jnp
from jax import lax
from jax.experimental import pallas as pl
from jax.experimental.pallas import tpu as pltpu

F32 = jnp.float32
BF16 = jnp.bfloat16

D_MODEL = 1024
D_FF = 2816
EPS = 1e-6
SSD_HEADS = 16
SSD_HEAD_DIM = 64
SSD_GROUPS = 2
SSD_STATE = 128
SSD_CONV = 4
SSD_CHUNK = 128
SSD_XBC = 1536
DA_HEADS = 8
DA_HEAD_DIM = 64
DA_EPS = 1e-5
LN_EPS = 1e-5
CONV_WIDTH = 31
IN_SIZES = (3 * D_MODEL, D_MODEL, SSD_XBC, SSD_HEADS, D_MODEL, D_MODEL, D_MODEL, 2 * D_MODEL)

LANES = 128
SUBLANES = 8
VMEM_LIMIT_BYTES = 56 * 1024 * 1024

P_GATES = 0
P_Z = 3072
P_PW = 4096
P_Q = 6144
P_K = 7168
P_V = 8192
P_XBC = 9216
P_COLS = 10752


def _params(*sem):
    return pltpu.CompilerParams(dimension_semantics=sem, vmem_limit_bytes=VMEM_LIMIT_BYTES)


def _resident(shape):
    nd = len(shape)
    return pl.BlockSpec(shape, lambda *_: (0,) * nd, pipeline_mode=pl.Buffered(1))


def _sigmoid(x):
    return 1.0 / (1.0 + jnp.exp(-x))


def _silu(x):
    return x * _sigmoid(x)


def _ffn_kernel(*refs, f_chunks, mode):
    if mode == "mix":
        x_ref, g_ref, w13_ref, w2_ref, gm_ref, xo_ref, h_ref = refs
    elif mode == "final":
        x_ref, g_ref, w13_ref, w2_ref, gf_ref, xo_ref = refs
    else:
        x_ref, g_ref, w13_ref, w2_ref, xo_ref = refs
    x = x_ref[...]
    h = (x * lax.rsqrt(jnp.mean(x * x, axis=-1, keepdims=True) + EPS) * g_ref[...]).astype(BF16)
    acc = jnp.zeros(x.shape, F32)
    lo = 0
    for width in f_chunks:
        gate = jnp.dot(h, w13_ref[:, lo:lo + width], preferred_element_type=F32)
        up = jnp.dot(h, w13_ref[:, D_FF + lo:D_FF + lo + width], preferred_element_type=F32)
        act = (_silu(gate) * up).astype(BF16)
        acc = acc + jnp.dot(act, w2_ref[lo:lo + width, :], preferred_element_type=F32)
        lo += width
    xn = x + 0.5 * acc
    if mode == "final":
        xo_ref[...] = xn * lax.rsqrt(jnp.mean(xn * xn, axis=-1, keepdims=True) + EPS) * gf_ref[...]
        return
    xo_ref[...] = xn
    if mode == "mix":
        hm = xn * lax.rsqrt(jnp.mean(xn * xn, axis=-1, keepdims=True) + EPS) * gm_ref[...]
        h_ref[...] = hm.astype(h_ref.dtype)


MXU_TILE = 256
FFN_CHUNKS = (6 * MXU_TILE, D_FF - 6 * MXU_TILE)


def _ffn(x, g, w13, w2, *, mode, extra=(), tm=512, f_chunks=FFN_CHUNKS):
    t, d = x.shape
    row = lambda i: (i, 0)
    in_specs = [pl.BlockSpec((tm, d), row), _resident((1, d)), _resident(w13.shape), _resident(w2.shape)]
    out_shape = [jax.ShapeDtypeStruct((t, d), F32)]
    out_specs = [pl.BlockSpec((tm, d), row)]
    if mode == "mix":
        in_specs += [_resident((1, d))]
        out_shape += [jax.ShapeDtypeStruct((t, d), BF16)]
        out_specs += [pl.BlockSpec((tm, d), row)]
    elif mode == "final":
        in_specs += [_resident((1, d))]
    out = pl.pallas_call(
        functools.partial(_ffn_kernel, f_chunks=f_chunks, mode=mode),
        grid=(t // tm,),
        in_specs=in_specs,
        out_specs=out_specs,
        out_shape=out_shape,
        compiler_params=_params("parallel"),
    )(x, g, w13, w2, *extra)
    return out if mode == "mix" else out[0]


def _proj_kernel(h_ref, w_ref, o_ref):
    o_ref[...] = jnp.dot(h_ref[...], w_ref[...], preferred_element_type=F32).astype(o_ref.dtype)


def _proj(h, w, *, tm=1024, tn=1792):
    t, d = h.shape
    n = w.shape[1]
    return pl.pallas_call(
        _proj_kernel,
        grid=(n // tn, t // tm),
        in_specs=[pl.BlockSpec((tm, d), lambda j, i: (i, 0)), pl.BlockSpec((d, tn), lambda j, i: (0, j))],
        out_specs=pl.BlockSpec((tm, tn), lambda j, i: (i, j)),
        out_shape=jax.ShapeDtypeStruct((t, n), BF16),
        compiler_params=_params("parallel", "parallel"),
    )(h, w)


def _ssd_kernel(xbc_ref, z_ref, h_ref, wdt_ref, cw_ref, cb_ref, dtb_ref, aneg_ref, dexp_ref, nrm_ref,
                expand_ref, o_ref, xwin_ref, state_ref):
    c = pl.program_id(1)
    L = SSD_CHUNK
    tail = SUBLANES

    @pl.when(c == 0)
    def _():
        xwin_ref[0:tail, :] = jnp.zeros((tail, SSD_XBC), F32)
        state_ref[...] = jnp.zeros(state_ref.shape, F32)

    @pl.when(c != 0)
    def _():
        xwin_ref[0:tail, :] = xwin_ref[L:L + tail, :]

    xwin_ref[tail:tail + L, :] = xbc_ref[...].astype(F32)
    win = xwin_ref[0:tail + L, :]
    conv = cb_ref[...] + cw_ref[SSD_CONV - 1:SSD_CONV, :] * win[tail:tail + L]
    for j in range(SSD_CONV - 1):
        shifted = pltpu.roll(win, SSD_CONV - 1 - j, axis=0)
        conv = conv + cw_ref[j:j + 1, :] * shifted[tail:tail + L]
    xbc = _silu(conv)
    xs = xbc[:, :D_MODEL]

    dt_in = jnp.dot(h_ref[...], wdt_ref[...], preferred_element_type=F32) + dtb_ref[...]
    dt = jnp.maximum(dt_in, 0.0) + jnp.log(1.0 + jnp.exp(-jnp.abs(dt_in)))
    a = dt * aneg_ref[...]
    row = lax.broadcasted_iota(jnp.int32, (L, L), 0)
    col = lax.broadcasted_iota(jnp.int32, (L, L), 1)
    causal = col <= row

    def split(v, terms):
        parts = []
        for _ in range(terms):
            piece = v.astype(BF16)
            parts.append(piece)
            v = v - piece.astype(F32)
        return parts

    tril = jnp.where(causal, 1.0, 0.0).astype(BF16)
    a_cum = jnp.dot(jnp.concatenate([tril] * 3, axis=1), jnp.concatenate(split(a, 3), axis=0),
                    preferred_element_type=F32)
    a_cum_t = a_cum.T
    a_last = a_cum[L - 1:L, :]
    exp_a = jnp.exp2(a_cum)
    dstate = jnp.exp2(a_last - a_cum)
    per_head = jnp.concatenate([jnp.concatenate(split(v, 2), axis=1) for v in (dt, dt * dstate, exp_a)], axis=0)
    per_chan = jnp.dot(per_head, expand_ref[...], preferred_element_type=F32)
    dt_x, dtd_x, expa_x = per_chan[0:L], per_chan[L:2 * L], per_chan[2 * L:3 * L]

    xdt = xs * dt_x
    xdt_state = (xs * dtd_x).astype(BF16)
    lane = lax.broadcasted_iota(jnp.int32, (L, D_MODEL), 1)
    lo_half = (lane & (LANES - 1)) < SSD_HEAD_DIM
    xdt_lo = jnp.where(lo_half, xdt, 0.0).astype(BF16)
    xdt_hi = jnp.where(lo_half, 0.0, xdt).astype(BF16)

    gw = SSD_STATE
    cg = D_MODEL // SSD_GROUPS
    hg = SSD_HEADS // SSD_GROUPS
    y_parts = []
    for g in range(SSD_GROUPS):
        bm = xbc[:, D_MODEL + g * gw:D_MODEL + (g + 1) * gw]
        cm = xbc[:, D_MODEL + SSD_GROUPS * gw + g * gw:D_MODEL + SSD_GROUPS * gw + (g + 1) * gw]
        bm16 = bm.astype(BF16)
        cm16 = cm.astype(BF16)
        cb = lax.dot_general(cm16, bm16, (((1,), (1,)), ((), ())), preferred_element_type=F32)
        for pair in range(hg // 2):
            ms = []
            for k in range(2):
                h = g * hg + pair * 2 + k
                seg = a_cum[:, h:h + 1] - a_cum_t[h:h + 1, :]
                ms.append((cb * jnp.exp2(jnp.where(causal, seg, -jnp.inf))).astype(BF16))
            m2 = jnp.concatenate(ms, axis=1)
            c0 = (g * hg + pair * 2) * SSD_HEAD_DIM
            rhs = jnp.concatenate([xdt_lo[:, c0:c0 + LANES], xdt_hi[:, c0:c0 + LANES]], axis=0)
            y_parts.append(jnp.dot(m2, rhs, preferred_element_type=F32))
    y = jnp.concatenate(y_parts, axis=1)

    outs = []
    for g in range(SSD_GROUPS):
        bm = xbc[:, D_MODEL + g * gw:D_MODEL + (g + 1) * gw]
        cm = xbc[:, D_MODEL + SSD_GROUPS * gw + g * gw:D_MODEL + SSD_GROUPS * gw + (g + 1) * gw]
        prev = state_ref[g]
        y_off = jnp.dot(cm.astype(BF16), prev.astype(BF16), preferred_element_type=F32)
        new = jnp.dot(bm.T.astype(BF16), xdt_state[:, g * cg:(g + 1) * cg], preferred_element_type=F32)
        state_ref[g] = prev * expa_x[L - 1:L, g * cg:(g + 1) * cg] + new
        yg = y[:, g * cg:(g + 1) * cg] + y_off * expa_x[:, g * cg:(g + 1) * cg]
        yg = yg + xs[:, g * cg:(g + 1) * cg] * dexp_ref[:, g * cg:(g + 1) * cg]
        u = yg * _silu(z_ref[:, g * cg:(g + 1) * cg].astype(F32))
        u = u * lax.rsqrt(jnp.mean(u * u, axis=-1, keepdims=True) + EPS)
        outs.append(u * nrm_ref[:, g * cg:(g + 1) * cg])
    o_ref[...] = jnp.concatenate(outs, axis=1).astype(o_ref.dtype)


def _ssd(p, h, wdt, cw, cb, dtb, aneg, dexp, nrm, expand, *, batch, seq):
    t = batch * seq
    nc = seq // SSD_CHUNK
    L = SSD_CHUNK
    rowmap = lambda b, c: (b * nc + c, 0)
    return pl.pallas_call(
        _ssd_kernel,
        grid=(batch, nc),
        in_specs=[
            pl.BlockSpec((L, SSD_XBC), lambda b, c: (b * nc + c, P_XBC // SSD_XBC)),
            pl.BlockSpec((L, D_MODEL), lambda b, c: (b * nc + c, P_Z // D_MODEL)),
            pl.BlockSpec((L, D_MODEL), rowmap), _resident(wdt.shape),
            _resident(cw.shape), _resident(cb.shape), _resident(dtb.shape), _resident(aneg.shape),
            _resident(dexp.shape), _resident(nrm.shape), _resident(expand.shape),
        ],
        out_specs=pl.BlockSpec((L, D_MODEL), rowmap),
        out_shape=jax.ShapeDtypeStruct((t, D_MODEL), BF16),
        scratch_shapes=[
            pltpu.VMEM((L + 2 * SUBLANES, SSD_XBC), F32),
            pltpu.VMEM((SSD_GROUPS, SSD_STATE, D_MODEL // SSD_GROUPS), F32),
        ],
        compiler_params=_params("parallel", "arbitrary"),
    )(p, p, h, wdt, cw, cb, dtb, aneg, dexp, nrm, expand)


def _attn_kernel(q_ref, k_ref, v_ref, lam_ref, sub_ref, o_ref, q2_ref, m_ref, acc_ref, s0_ref, s1_ref, *,
                 tq, tk, nq, lambda_init):
    per = tq // tk
    first = lax.broadcasted_iota(jnp.int32, (tk, LANES), 1) < DA_HEAD_DIM

    def stack(t, carry):
        q = q_ref[pl.ds(pl.multiple_of(t * tk, tk), tk), :]
        zero = jnp.zeros_like(q)
        dst = pl.multiple_of(2 * t * tk, tk)
        q2_ref[pl.ds(dst, tk), :] = jnp.where(first, q, zero)
        q2_ref[pl.ds(dst + tk, tk), :] = jnp.where(first, zero, q)
        return carry

    lax.fori_loop(0, nq * per, stack, 0)
    ones = jnp.ones((tk, LANES), BF16)
    nt = (((1,), (1,)), ((), ()))
    s_refs = (s0_ref, s1_ref)

    def scores(qi, j, dst_ref, lo):
        rows = pl.ds(pl.multiple_of(qi * 2 * tq + lo, tk), 2 * tq - lo)
        keys = pl.ds(pl.multiple_of(j * tk, tk), tk)
        dst_ref[lo:2 * tq, :] = lax.dot_general(q2_ref[rows, :], k_ref[keys, :], nt, preferred_element_type=F32)

    def consume(j, src_ref, diag):
        lo = 0 if diag is None else 2 * diag * tk
        start = pl.multiple_of(j * tk, tk)
        v2 = jnp.concatenate([v_ref[pl.ds(start, tk), :], ones], axis=1)
        s = src_ref[lo:2 * tq, :]
        if diag is not None:
            top = s[0:2 * tk]
            r = lax.broadcasted_iota(jnp.int32, top.shape, 0) & (tk - 1)
            cc = lax.broadcasted_iota(jnp.int32, top.shape, 1)
            top = jnp.where(cc <= r, top, -jnp.inf)
            s = top if diag == per - 1 else jnp.concatenate([top, s[2 * tk:]], axis=0)
        m_old = m_ref[lo:2 * tq, :]
        m_new = jnp.maximum(m_old, jnp.max(s, axis=-1, keepdims=True))
        alpha = jnp.exp2(m_old - m_new)
        p = jnp.exp2(s - jnp.concatenate([m_new] * (tk // LANES), axis=1))
        acc_ref[lo:2 * tq, :] = (jnp.concatenate([alpha, alpha], axis=1) * acc_ref[lo:2 * tq, :]
                                 + jnp.dot(p.astype(BF16), v2, preferred_element_type=F32))
        m_ref[lo:2 * tq, :] = m_new

    lp = lam_ref[...]
    lam = (jnp.exp(jnp.sum(lp[0:1] * lp[1:2], axis=-1, keepdims=True))
           - jnp.exp(jnp.sum(lp[2:3] * lp[3:4], axis=-1, keepdims=True)) + lambda_init)

    scores(0, 0, s_refs[0], 0)

    def query_tile(qi, carry):
        m_ref[...] = jnp.full(m_ref.shape, -jnp.inf, F32)
        acc_ref[...] = jnp.zeros(acc_ref.shape, F32)

        def tiles(first_tile, count):
            for d in range(count):
                j = first_tile + d
                scores(qi, j + 1, s_refs[(d + 1) % 2], 0)
                consume(j, s_refs[d % 2], None)

        def body(g, c):
            tiles(g * 2 * per, 2 * per)
            return c

        lax.fori_loop(0, qi // 2, body, 0)

        @pl.when(qi % 2 == 1)
        def _():
            tiles((qi - 1) * per, per)

        for d in range(per):
            j = qi * per + d
            if d + 1 < per:
                scores(qi, j + 1, s_refs[(d + 1) % 2], 2 * (d + 1) * tk)
            else:
                scores(jnp.minimum(qi + 1, nq - 1), 0, s_refs[0], 0)
            consume(j, s_refs[d % 2], d)

        for b in range(per):
            r0, r1 = 2 * b * tk, (2 * b + 1) * tk
            o = (acc_ref[r0:r0 + tk, 0:LANES] / acc_ref[r0:r0 + tk, LANES:2 * LANES]
                 - lam * (acc_ref[r1:r1 + tk, 0:LANES] / acc_ref[r1:r1 + tk, LANES:2 * LANES]))
            o = o * lax.rsqrt(jnp.mean(o * o, axis=-1, keepdims=True) + DA_EPS) * sub_ref[...]
            out_rows = pl.ds(pl.multiple_of(qi * tq + b * tk, tk), tk)
            o_ref[out_rows, :] = (o * (1.0 - lambda_init)).astype(o_ref.dtype)
        return carry

    lax.fori_loop(0, nq, query_tile, 0)


def _attn(p, lam_rows, subln, *, batch, seq, lambda_init, tq=1024, tk=512):
    t = batch * seq
    tq = min(tq, seq)
    assert tk & (tk - 1) == 0 and seq % tq == 0 and tq % (2 * tk) == 0
    nq = seq // tq
    head_block = lambda col: pl.BlockSpec((seq, LANES), lambda b, h: (b, col // LANES + h))
    return pl.pallas_call(
        functools.partial(_attn_kernel, tq=tq, tk=tk, nq=nq, lambda_init=lambda_init),
        grid=(batch, DA_HEADS),
        in_specs=[head_block(P_Q), head_block(P_K), head_block(P_V),
                  _resident(lam_rows.shape), _resident(subln.shape)],
        out_specs=pl.BlockSpec((seq, LANES), lambda b, h: (b, h)),
        out_shape=jax.ShapeDtypeStruct((t, D_MODEL), BF16),
        scratch_shapes=[
            pltpu.VMEM((2 * seq, LANES), BF16), pltpu.VMEM((2 * tq, LANES), F32),
            pltpu.VMEM((2 * tq, 2 * LANES), F32),
            pltpu.VMEM((2 * tq, tk), F32), pltpu.VMEM((2 * tq, tk), F32),
        ],
        compiler_params=_params("parallel", "parallel"),
    )(p, p, p, lam_rows, subln)


def _conv_kernel(pw_ref, w_ref, b_ref, g_ref, beta_ref, o_ref, win_ref, y_ref, *, tm, rows, lanes):
    i = pl.program_id(1)
    halo = 4 * SUBLANES

    @pl.when(i == 0)
    def _():
        win_ref[0:halo, :] = jnp.zeros((halo, D_MODEL), F32)

    @pl.when(i != 0)
    def _():
        win_ref[0:halo, :] = win_ref[tm:tm + halo, :]

    pa = pw_ref[:, :D_MODEL].astype(F32)
    pg = pw_ref[:, D_MODEL:].astype(F32)
    win_ref[halo:halo + tm, :] = pa * _sigmoid(pg)
    first = halo - (CONV_WIDTH - 1)
    for r0 in range(0, tm, rows):
        for c0 in range(0, D_MODEL, lanes):
            cols = slice(c0, c0 + lanes)
            acc = jnp.broadcast_to(b_ref[:, cols], (rows, lanes))
            for r in range(SUBLANES):
                n = rows if r == 0 else rows + SUBLANES
                z = None
                for o in range(r, halo + 1, SUBLANES):
                    if o < first:
                        continue
                    j = o - first
                    term = w_ref[j:j + 1, cols] * win_ref[r0 + o - r:r0 + o - r + n, cols]
                    z = term if z is None else z + term
                acc = acc + z[r:r + rows]
            y_ref[r0:r0 + rows, cols] = acc
    for r0 in range(0, tm, rows):
        acc = y_ref[r0:r0 + rows, :]
        mu = jnp.mean(acc, axis=-1, keepdims=True)
        d = acc - mu
        var = jnp.mean(d * d, axis=-1, keepdims=True)
        y = d * lax.rsqrt(var + LN_EPS) * g_ref[...] + beta_ref[...]
        o_ref[r0:r0 + rows, :] = _silu(y).astype(o_ref.dtype)


def _conv(p, w, b, g, beta, *, batch, seq, tm=256, rows=128, lanes=128):
    t = batch * seq
    nt = seq // tm
    return pl.pallas_call(
        functools.partial(_conv_kernel, tm=tm, rows=rows, lanes=lanes),
        grid=(batch, nt),
        in_specs=[
            pl.BlockSpec((tm, 2 * D_MODEL), lambda bb, i: (bb * nt + i, P_PW // (2 * D_MODEL))),
            _resident(w.shape), _resident(b.shape), _resident(g.shape), _resident(beta.shape),
        ],
        out_specs=pl.BlockSpec((tm, D_MODEL), lambda bb, i: (bb * nt + i, 0)),
        out_shape=jax.ShapeDtypeStruct((t, D_MODEL), BF16),
        scratch_shapes=[pltpu.VMEM((tm + 4 * SUBLANES, D_MODEL), F32), pltpu.VMEM((tm, D_MODEL), F32)],
        compiler_params=_params("parallel", "arbitrary"),
    )(p, w, b, g, beta)


def _merge_kernel(x_ref, ua_ref, ub_ref, uc_ref, gt_ref, bg_ref, wa_ref, wb_ref, wc_ref, wo_ref, o_ref):
    d = D_MODEL
    merged = None
    for n, (u_ref, w_ref) in enumerate(((ua_ref, wa_ref), (ub_ref, wb_ref), (uc_ref, wc_ref))):
        y = jnp.dot(u_ref[...], w_ref[...], preferred_element_type=F32)
        gate = _sigmoid(gt_ref[:, n * d:(n + 1) * d].astype(F32) + bg_ref[:, n * d:(n + 1) * d])
        merged = gate * y if merged is None else merged + gate * y
    o_ref[...] = x_ref[...] + jnp.dot(merged.astype(BF16), wo_ref[...], preferred_element_type=F32)


def _merge(x, ua, ub, uc, p, bg, wa, wb, wc, wo, *, tm=512):
    t, d = x.shape
    row = lambda i: (i, 0)
    act = pl.BlockSpec((tm, d), row)
    return pl.pallas_call(
        _merge_kernel,
        grid=(t // tm,),
        in_specs=[act, act, act, act, pl.BlockSpec((tm, 3 * d), lambda i: (i, P_GATES // (3 * d))),
                  _resident(bg.shape), _resident(wa.shape), _resident(wb.shape), _resident(wc.shape),
                  _resident(wo.shape)],
        out_specs=act,
        out_shape=jax.ShapeDtypeStruct((t, d), F32),
        compiler_params=_params("parallel"),
    )(x, ua, ub, uc, p, bg, wa, wb, wc, wo)


def _split_w_in(w):
    pts = [0]
    for s in IN_SIZES:
        pts.append(pts[-1] + s)
    gates, z, xbc, dt, q, k, v, pw = (w[:, pts[n]:pts[n + 1]] for n in range(len(IN_SIZES)))
    q = q * (DA_HEAD_DIM ** -0.5 * math.log2(math.e))
    w_main = jnp.concatenate([gates, z, pw, q, k, v, xbc], axis=1).astype(BF16)
    w_dt = jnp.pad(dt, ((0, 0), (0, LANES - SSD_HEADS))).astype(BF16)
    return w_main, w_dt


def _pad_lanes(v):
    return jnp.pad(v, (0, LANES - v.shape[0]))[None, :]


def kernel(x, ffn1_norm, ffn1_w13, ffn1_w2, mix_norm, w_in, b_gate, ssd_conv_w, ssd_conv_b, ssd_dt_bias,
           ssd_a_log, ssd_d, ssd_norm, ssd_wo, da_lq1, da_lk1, da_lq2, da_lk2, da_subln, da_wo, cv_dw_w,
           cv_dw_b, cv_ln_g, cv_ln_b, cv_wo, w_out, ffn2_norm, ffn2_w13, ffn2_w2, final_norm):
    batch, seq, d = x.shape
    depth = ffn1_norm.shape[0]
    xf = x.reshape(batch * seq, d)
    head_of_channel = jnp.arange(D_MODEL, dtype=jnp.int32) // SSD_HEAD_DIM
    expand = (jnp.arange(LANES, dtype=jnp.int32)[:, None] == head_of_channel[None, :]).astype(BF16)
    expand = jnp.concatenate([expand, expand], axis=0)
    for l in range(depth):
        lambda_init = 0.8 - 0.6 * math.exp(-0.3 * l)
        w_main, w_dt = _split_w_in(w_in[l])
        xf, h = _ffn(xf, ffn1_norm[l][None, :], ffn1_w13[l].astype(BF16), ffn1_w2[l].astype(BF16),
                     mode="mix", extra=(mix_norm[l][None, :],))
        p = _proj(h, w_main)
        ua = _ssd(p, h, w_dt, ssd_conv_w[l], ssd_conv_b[l][None, :], _pad_lanes(ssd_dt_bias[l]),
                  _pad_lanes(-jnp.exp(ssd_a_log[l]) * math.log2(math.e)),
                  jnp.repeat(ssd_d[l], SSD_HEAD_DIM)[None, :],
                  ssd_norm[l][None, :], expand, batch=batch, seq=seq)
        lam_rows = jnp.pad(jnp.stack([da_lq1[l], da_lk1[l], da_lq2[l], da_lk2[l]]),
                           ((0, SUBLANES - 4), (0, LANES - DA_HEAD_DIM)))
        ub = _attn(p, lam_rows, da_subln[l][None, :], batch=batch, seq=seq, lambda_init=lambda_init)
        uc = _conv(p, cv_dw_w[l], cv_dw_b[l][None, :], cv_ln_g[l][None, :], cv_ln_b[l][None, :],
                   batch=batch, seq=seq)
        xf = _merge(xf, ua, ub, uc, p, b_gate[l][None, :], ssd_wo[l].astype(BF16), da_wo[l].astype(BF16),
                    cv_wo[l].astype(BF16), w_out[l].astype(BF16))
        if l == depth - 1:
            xf = _ffn(xf, ffn2_norm[l][None, :], ffn2_w13[l].astype(BF16), ffn2_w2[l].astype(BF16),
                      mode="final", extra=(final_norm[None, :],))
        else:
            xf = _ffn(xf, ffn2_norm[l][None, :], ffn2_w13[l].astype(BF16), ffn2_w2[l].astype(BF16),
                      mode="plain")
    return xf.reshape(batch, seq, d)
```

```python
import functools
import math

import jax
import jax.numpy as jnp
from jax import lax
from jax.experimental import pallas as pl
from jax.experimental.pallas import tpu as pltpu

F32 = jnp.float32
BF16 = jnp.bfloat16

D_MODEL = 1024
D_FF = 2816
EPS = 1e-6
SSD_HEADS = 16
SSD_HEAD_DIM = 64
SSD_GROUPS = 2
SSD_STATE = 128
SSD_CONV = 4
SSD_CHUNK = 128
SSD_XBC = 1536
DA_HEADS = 8
DA_HEAD_DIM = 64
DA_EPS = 1e-5
LN_EPS = 1e-5
CONV_WIDTH = 31
IN_SIZES = (3 * D_MODEL, D_MODEL, SSD_XBC, SSD_HEADS, D_MODEL, D_MODEL, D_MODEL, 2 * D_MODEL)

LANES = 128
SUBLANES = 8
VMEM_LIMIT_BYTES = 56 * 1024 * 1024

P_GATES = 0
P_Z = 3072
P_PW = 4096
P_Q = 6144
P_K = 7168
P_V = 8192
P_XBC = 9216
P_COLS = 10752


def _params(*sem):
    return pltpu.CompilerParams(dimension_semantics=sem, vmem_limit_bytes=VMEM_LIMIT_BYTES)


def _resident(shape):
    nd = len(shape)
    return pl.BlockSpec(shape, lambda *_: (0,) * nd, pipeline_mode=pl.Buffered(1))


def _sigmoid(x):
    return 1.0 / (1.0 + jnp.exp(-x))


def _silu(x):
    return x * _sigmoid(x)


def _ffn_kernel(*refs, f_chunks, mode):
    if mode == "mix":
        x_ref, g_ref, w13_ref, w2_ref, gm_ref, xo_ref, h_ref = refs
    elif mode == "final":
        x_ref, g_ref, w13_ref, w2_ref, gf_ref, xo_ref = refs
    else:
        x_ref, g_ref, w13_ref, w2_ref, xo_ref = refs
    x = x_ref[...]
    h = (x * lax.rsqrt(jnp.mean(x * x, axis=-1, keepdims=True) + EPS) * g_ref[...]).astype(BF16)
    acc = jnp.zeros(x.shape, F32)
    lo = 0
    for width in f_chunks:
        gate = jnp.dot(h, w13_ref[:, lo:lo + width], preferred_element_type=F32)
        up = jnp.dot(h, w13_ref[:, D_FF + lo:D_FF + lo + width], preferred_element_type=F32)
        act = (_silu(gate) * up).astype(BF16)
        acc = acc + jnp.dot(act, w2_ref[lo:lo + width, :], preferred_element_type=F32)
        lo += width
    xn = x + 0.5 * acc
    if mode == "final":
        xo_ref[...] = xn * lax.rsqrt(jnp.mean(xn * xn, axis=-1, keepdims=True) + EPS) * gf_ref[...]
        return
    xo_ref[...] = xn
    if mode == "mix":
        hm = xn * lax.rsqrt(jnp.mean(xn * xn, axis=-1, keepdims=True) + EPS) * gm_ref[...]
        h_ref[...] = hm.astype(h_ref.dtype)


MXU_TILE = 256
FFN_CHUNKS = (6 * MXU_TILE, D_FF - 6 * MXU_TILE)


def _ffn(x, g, w13, w2, *, mode, extra=(), tm=512, f_chunks=FFN_CHUNKS):
    t, d = x.shape
    row = lambda i: (i, 0)
    in_specs = [pl.BlockSpec((tm, d), row), _resident((1, d)), _resident(w13.shape), _resident(w2.shape)]
    out_shape = [jax.ShapeDtypeStruct((t, d), F32)]
    out_specs = [pl.BlockSpec((tm, d), row)]
    if mode == "mix":
        in_specs += [_resident((1, d))]
        out_shape += [jax.ShapeDtypeStruct((t, d), BF16)]
        out_specs += [pl.BlockSpec((tm, d), row)]
    elif mode == "final":
        in_specs += [_resident((1, d))]
    out = pl.pallas_call(
        functools.partial(_ffn_kernel, f_chunks=f_chunks, mode=mode),
        grid=(t // tm,),
        in_specs=in_specs,
        out_specs=out_specs,
        out_shape=out_shape,
        compiler_params=_params("parallel"),
    )(x, g, w13, w2, *extra)
    return out if mode == "mix" else out[0]


def _proj_kernel(h_ref, w_ref, o_ref):
    o_ref[...] = jnp.dot(h_ref[...], w_ref[...], preferred_element_type=F32).astype(o_ref.dtype)


def _proj(h, w, *, tm=1024, tn=1792):
    t, d = h.shape
    n = w.shape[1]
    return pl.pallas_call(
        _proj_kernel,
        grid=(n // tn, t // tm),
        in_specs=[pl.BlockSpec((tm, d), lambda j, i: (i, 0)), pl.BlockSpec((d, tn), lambda j, i: (0, j))],
        out_specs=pl.BlockSpec((tm, tn), lambda j, i: (i, j)),
        out_shape=jax.ShapeDtypeStruct((t, n), BF16),
        compiler_params=_params("parallel", "parallel"),
    )(h, w)


def _ssd_kernel(xbc_ref, z_ref, h_ref, wdt_ref, cw_ref, cb_ref, dtb_ref, aneg_ref, dexp_ref, nrm_ref,
                expand_ref, o_ref, xwin_ref, state_ref):
    c = pl.program_id(1)
    L = SSD_CHUNK
    tail = SUBLANES
    rows_all = xbc_ref.shape[0]

    @pl.when(c == 0)
    def _():
        xwin_ref[0:tail, :] = jnp.zeros((tail, SSD_XBC), F32)
        state_ref[...] = jnp.zeros(state_ref.shape, F32)

    @pl.when(c != 0)
    def _():
        xwin_ref[0:tail, :] = xwin_ref[rows_all:rows_all + tail, :]

    xwin_ref[tail:tail + rows_all, :] = xbc_ref[...].astype(F32)
    win = xwin_ref[0:tail + rows_all, :]
    conv = cb_ref[...] + cw_ref[SSD_CONV - 1:SSD_CONV, :] * win[tail:tail + rows_all]
    for j in range(SSD_CONV - 1):
        shifted = pltpu.roll(win, SSD_CONV - 1 - j, axis=0)
        conv = conv + cw_ref[j:j + 1, :] * shifted[tail:tail + rows_all]
    xbc_all = _silu(conv)

    dt_in = jnp.dot(h_ref[...], wdt_ref[...], preferred_element_type=F32) + dtb_ref[...]
    dt_all = jnp.maximum(dt_in, 0.0) + jnp.log(1.0 + jnp.exp(-jnp.abs(dt_in)))
    a_all = dt_all * aneg_ref[...]
    row = lax.broadcasted_iota(jnp.int32, (L, L), 0)
    col = lax.broadcasted_iota(jnp.int32, (L, L), 1)
    causal = col <= row
    for r0 in range(0, rows_all, L):
        _ssd_chunk(xbc_all[r0:r0 + L], dt_all[r0:r0 + L], a_all[r0:r0 + L], causal,
                   z_ref.at[r0:r0 + L], o_ref.at[r0:r0 + L], dexp_ref, nrm_ref, expand_ref, state_ref)


def _ssd_chunk(xbc, dt, a, causal, z_ref, o_ref, dexp_ref, nrm_ref, expand_ref, state_ref):
    L = SSD_CHUNK
    xs = xbc[:, :D_MODEL]

    def split(v, terms):
        parts = []
        for _ in range(terms):
            piece = v.astype(BF16)
            parts.append(piece)
            v = v - piece.astype(F32)
        return parts

    tril = jnp.where(causal, 1.0, 0.0).astype(BF16)
    a_cum = jnp.dot(jnp.concatenate([tril] * 3, axis=1), jnp.concatenate(split(a, 3), axis=0),
                    preferred_element_type=F32)
    a_cum_t = a_cum.T
    a_last = a_cum[L - 1:L, :]
    exp_a = jnp.exp2(a_cum)
    dstate = jnp.exp2(a_last - a_cum)
    per_head = jnp.concatenate([jnp.concatenate(split(v, 2), axis=1) for v in (dt, dt * dstate, exp_a)], axis=0)
    per_chan = jnp.dot(per_head, expand_ref[...], preferred_element_type=F32)
    dt_x, dtd_x, expa_x = per_chan[0:L], per_chan[L:2 * L], per_chan[2 * L:3 * L]

    xdt = xs * dt_x
    xdt_state = (xs * dtd_x).astype(BF16)
    lane = lax.broadcasted_iota(jnp.int32, (L, D_MODEL), 1)
    lo_half = (lane & (LANES - 1)) < SSD_HEAD_DIM
    xdt_lo = jnp.where(lo_half, xdt, 0.0).astype(BF16)
    xdt_hi = jnp.where(lo_half, 0.0, xdt).astype(BF16)

    gw = SSD_STATE
    cg = D_MODEL // SSD_GROUPS
    hg = SSD_HEADS // SSD_GROUPS
    y_parts = []
    for g in range(SSD_GROUPS):
        bm = xbc[:, D_MODEL + g * gw:D_MODEL + (g + 1) * gw]
        cm = xbc[:, D_MODEL + SSD_GROUPS * gw + g * gw:D_MODEL + SSD_GROUPS * gw + (g + 1) * gw]
        bm16 = bm.astype(BF16)
        cm16 = cm.astype(BF16)
        cb = lax.dot_general(cm16, bm16, (((1,), (1,)), ((), ())), preferred_element_type=F32)
        for pair in range(hg // 2):
            ms = []
            for k in range(2):
                h = g * hg + pair * 2 + k
                seg = a_cum[:, h:h + 1] - a_cum_t[h:h + 1, :]
                ms.append((cb * jnp.exp2(jnp.where(causal, seg, -jnp.inf))).astype(BF16))
            m2 = jnp.concatenate(ms, axis=1)
            c0 = (g * hg + pair * 2) * SSD_HEAD_DIM
            rhs = jnp.concatenate([xdt_lo[:, c0:c0 + LANES], xdt_hi[:, c0:c0 + LANES]], axis=0)
            y_parts.append(jnp.dot(m2, rhs, preferred_element_type=F32))
    y = jnp.concatenate(y_parts, axis=1)

    outs = []
    for g in range(SSD_GROUPS):
        bm = xbc[:, D_MODEL + g * gw:D_MODEL + (g + 1) * gw]
        cm = xbc[:, D_MODEL + SSD_GROUPS * gw + g * gw:D_MODEL + SSD_GROUPS * gw + (g + 1) * gw]
        prev = state_ref[g]
        y_off = jnp.dot(cm.astype(BF16), prev.astype(BF16), preferred_element_type=F32)
        new = jnp.dot(bm.T.astype(BF16), xdt_state[:, g * cg:(g + 1) * cg], preferred_element_type=F32)
        state_ref[g] = prev * expa_x[L - 1:L, g * cg:(g + 1) * cg] + new
        yg = y[:, g * cg:(g + 1) * cg] + y_off * expa_x[:, g * cg:(g + 1) * cg]
        yg = yg + xs[:, g * cg:(g + 1) * cg] * dexp_ref[:, g * cg:(g + 1) * cg]
        u = yg * _silu(z_ref[:, g * cg:(g + 1) * cg].astype(F32))
        u = u * lax.rsqrt(jnp.mean(u * u, axis=-1, keepdims=True) + EPS)
        outs.append(u * nrm_ref[:, g * cg:(g + 1) * cg])
    o_ref[...] = jnp.concatenate(outs, axis=1).astype(o_ref.dtype)


def _ssd(p, h, wdt, cw, cb, dtb, aneg, dexp, nrm, expand, *, batch, seq, chunks_per_step=2):
    t = batch * seq
    L = SSD_CHUNK * chunks_per_step
    nc = seq // L
    rowmap = lambda b, c: (b * nc + c, 0)
    return pl.pallas_call(
        _ssd_kernel,
        grid=(batch, nc),
        in_specs=[
            pl.BlockSpec((L, SSD_XBC), lambda b, c: (b * nc + c, P_XBC // SSD_XBC)),
            pl.BlockSpec((L, D_MODEL), lambda b, c: (b * nc + c, P_Z // D_MODEL)),
            pl.BlockSpec((L, D_MODEL), rowmap), _resident(wdt.shape),
            _resident(cw.shape), _resident(cb.shape), _resident(dtb.shape), _resident(aneg.shape),
            _resident(dexp.shape), _resident(nrm.shape), _resident(expand.shape),
        ],
        out_specs=pl.BlockSpec((L, D_MODEL), rowmap),
        out_shape=jax.ShapeDtypeStruct((t, D_MODEL), BF16),
        scratch_shapes=[
            pltpu.VMEM((L + 2 * SUBLANES, SSD_XBC), F32),
            pltpu.VMEM((SSD_GROUPS, SSD_STATE, D_MODEL // SSD_GROUPS), F32),
        ],
        compiler_params=_params("parallel", "arbitrary"),
    )(p, p, h, wdt, cw, cb, dtb, aneg, dexp, nrm, expand)


def _attn_kernel(q_ref, k_ref, v_ref, lam_ref, sub_ref, o_ref, q2_ref, m_ref, acc_ref, s0_ref, s1_ref, *,
                 tq, tk, nq, lambda_init):
    per = tq // tk
    first = lax.broadcasted_iota(jnp.int32, (tk, LANES), 1) < DA_HEAD_DIM

    def stack(t, carry):
        q = q_ref[pl.ds(pl.multiple_of(t * tk, tk), tk), :]
        zero = jnp.zeros_like(q)
        dst = pl.multiple_of(2 * t * tk, tk)
        q2_ref[pl.ds(dst, tk), :] = jnp.where(first, q, zero)
        q2_ref[pl.ds(dst + tk, tk), :] = jnp.where(first, zero, q)
        return carry

    lax.fori_loop(0, nq * per, stack, 0)
    ones = jnp.ones((tk, LANES), BF16)
    nt = (((1,), (1,)), ((), ()))
    s_refs = (s0_ref, s1_ref)

    def scores(qi, j, dst_ref, lo):
        rows = pl.ds(pl.multiple_of(qi * 2 * tq + lo, tk), 2 * tq - lo)
        keys = pl.ds(pl.multiple_of(j * tk, tk), tk)
        dst_ref[lo:2 * tq, :] = lax.dot_general(q2_ref[rows, :], k_ref[keys, :], nt, preferred_element_type=F32)

    def consume(j, src_ref, diag):
        lo = 0 if diag is None else 2 * diag * tk
        start = pl.multiple_of(j * tk, tk)
        v2 = jnp.concatenate([v_ref[pl.ds(start, tk), :], ones], axis=1)
        s = src_ref[lo:2 * tq, :]
        if diag is not None:
            top = s[0:2 * tk]
            r = lax.broadcasted_iota(jnp.int32, top.shape, 0) & (tk - 1)
            cc = lax.broadcasted_iota(jnp.int32, top.shape, 1)
            top = jnp.where(cc <= r, top, -jnp.inf)
            s = top if diag == per - 1 else jnp.concatenate([top, s[2 * tk:]], axis=0)
        m_old = m_ref[lo:2 * tq, :]
        m_new = jnp.maximum(m_old, jnp.max(s, axis=-1, keepdims=True))
        alpha = jnp.exp2(m_old - m_new)
        p = jnp.exp2(s - jnp.concatenate([m_new] * (tk // LANES), axis=1))
        acc_ref[lo:2 * tq, :] = (jnp.concatenate([alpha, alpha], axis=1) * acc_ref[lo:2 * tq, :]
                                 + jnp.dot(p.astype(BF16), v2, preferred_element_type=F32))
        m_ref[lo:2 * tq, :] = m_new

    lp = lam_ref[...]
    lam = (jnp.exp(jnp.sum(lp[0:1] * lp[1:2], axis=-1, keepdims=True))
           - jnp.exp(jnp.sum(lp[2:3] * lp[3:4], axis=-1, keepdims=True)) + lambda_init)

    scores(0, 0, s_refs[0], 0)

    def query_tile(qi, carry):
        m_ref[...] = jnp.full(m_ref.shape, -jnp.inf, F32)
        acc_ref[...] = jnp.zeros(acc_ref.shape, F32)

        def tiles(first_tile, count):
            for d in range(count):
                j = first_tile + d
                scores(qi, j + 1, s_refs[(d + 1) % 2], 0)
                consume(j, s_refs[d % 2], None)

        def body(g, c):
            tiles(g * 2 * per, 2 * per)
            return c

        lax.fori_loop(0, qi // 2, body, 0)

        @pl.when(qi % 2 == 1)
        def _():
            tiles((qi - 1) * per, per)

        for d in range(per):
            j = qi * per + d
            if d + 1 < per:
                scores(qi, j + 1, s_refs[(d + 1) % 2], 2 * (d + 1) * tk)
            else:
                scores(jnp.minimum(qi + 1, nq - 1), 0, s_refs[0], 0)
            consume(j, s_refs[d % 2], d)

        for b in range(per):
            r0, r1 = 2 * b * tk, (2 * b + 1) * tk
            o = (acc_ref[r0:r0 + tk, 0:LANES] / acc_ref[r0:r0 + tk, LANES:2 * LANES]
                 - lam * (acc_ref[r1:r1 + tk, 0:LANES] / acc_ref[r1:r1 + tk, LANES:2 * LANES]))
            o = o * lax.rsqrt(jnp.mean(o * o, axis=-1, keepdims=True) + DA_EPS) * sub_ref[...]
            out_rows = pl.ds(pl.multiple_of(qi * tq + b * tk, tk), tk)
            o_ref[out_rows, :] = (o * (1.0 - lambda_init)).astype(o_ref.dtype)
        return carry

    lax.fori_loop(0, nq, query_tile, 0)


def _attn(p, lam_rows, subln, *, batch, seq, lambda_init, tq=1024, tk=512):
    t = batch * seq
    tq = min(tq, seq)
    assert tk & (tk - 1) == 0 and seq % tq == 0 and tq % (2 * tk) == 0
    nq = seq // tq
    head_block = lambda col: pl.BlockSpec((seq, LANES), lambda b, h: (b, col // LANES + h))
    return pl.pallas_call(
        functools.partial(_attn_kernel, tq=tq, tk=tk, nq=nq, lambda_init=lambda_init),
        grid=(batch, DA_HEADS),
        in_specs=[head_block(P_Q), head_block(P_K), head_block(P_V),
                  _resident(lam_rows.shape), _resident(subln.shape)],
        out_specs=pl.BlockSpec((seq, LANES), lambda b, h: (b, h)),
        out_shape=jax.ShapeDtypeStruct((t, D_MODEL), BF16),
        scratch_shapes=[
            pltpu.VMEM((2 * seq, LANES), BF16), pltpu.VMEM((2 * tq, LANES), F32),
            pltpu.VMEM((2 * tq, 2 * LANES), F32),
            pltpu.VMEM((2 * tq, tk), F32), pltpu.VMEM((2 * tq, tk), F32),
        ],
        compiler_params=_params("parallel", "parallel"),
    )(p, p, p, lam_rows, subln)


def _conv_kernel(pw_ref, w_ref, b_ref, g_ref, beta_ref, o_ref, win_ref, y_ref, *, tm, rows, lanes):
    i = pl.program_id(1)
    halo = 4 * SUBLANES

    @pl.when(i == 0)
    def _():
        win_ref[0:halo, :] = jnp.zeros((halo, D_MODEL), F32)

    @pl.when(i != 0)
    def _():
        win_ref[0:halo, :] = win_ref[tm:tm + halo, :]

    pa = pw_ref[:, :D_MODEL].astype(F32)
    pg = pw_ref[:, D_MODEL:].astype(F32)
    win_ref[halo:halo + tm, :] = pa * _sigmoid(pg)
    first = halo - (CONV_WIDTH - 1)
    for r0 in range(0, tm, rows):
        for c0 in range(0, D_MODEL, lanes):
            cols = slice(c0, c0 + lanes)
            acc = jnp.broadcast_to(b_ref[:, cols], (rows, lanes))
            for r in range(SUBLANES):
                n = rows if r == 0 else rows + SUBLANES
                z = None
                for o in range(r, halo + 1, SUBLANES):
                    if o < first:
                        continue
                    j = o - first
                    term = w_ref[j:j + 1, cols] * win_ref[r0 + o - r:r0 + o - r + n, cols]
                    z = term if z is None else z + term
                acc = acc + z[r:r + rows]
            y_ref[r0:r0 + rows, cols] = acc
    for r0 in range(0, tm, rows):
        acc = y_ref[r0:r0 + rows, :]
        mu = jnp.mean(acc, axis=-1, keepdims=True)
        d = acc - mu
        var = jnp.mean(d * d, axis=-1, keepdims=True)
        y = d * lax.rsqrt(var + LN_EPS) * g_ref[...] + beta_ref[...]
        o_ref[r0:r0 + rows, :] = _silu(y).astype(o_ref.dtype)


def _conv(p, w, b, g, beta, *, batch, seq, tm=256, rows=128, lanes=128):
    t = batch * seq
    nt = seq // tm
    return pl.pallas_call(
        functools.partial(_conv_kernel, tm=tm, rows=rows, lanes=lanes),
        grid=(batch, nt),
        in_specs=[
            pl.BlockSpec((tm, 2 * D_MODEL), lambda bb, i: (bb * nt + i, P_PW // (2 * D_MODEL))),
            _resident(w.shape), _resident(b.shape), _resident(g.shape), _resident(beta.shape),
        ],
        out_specs=pl.BlockSpec((tm, D_MODEL), lambda bb, i: (bb * nt + i, 0)),
        out_shape=jax.ShapeDtypeStruct((t, D_MODEL), BF16),
        scratch_shapes=[pltpu.VMEM((tm + 4 * SUBLANES, D_MODEL), F32), pltpu.VMEM((tm, D_MODEL), F32)],
        compiler_params=_params("parallel", "arbitrary"),
    )(p, w, b, g, beta)


def _merge_kernel(x_ref, ua_ref, ub_ref, uc_ref, gt_ref, bg_ref, wa_ref, wb_ref, wc_ref, wo_ref, o_ref):
    d = D_MODEL
    merged = None
    for n, (u_ref, w_ref) in enumerate(((ua_ref, wa_ref), (ub_ref, wb_ref), (uc_ref, wc_ref))):
        y = jnp.dot(u_ref[...], w_ref[...], preferred_element_type=F32)
        gate = _sigmoid(gt_ref[:, n * d:(n + 1) * d].astype(F32) + bg_ref[:, n * d:(n + 1) * d])
        merged = gate * y if merged is None else merged + gate * y
    o_ref[...] = x_ref[...] + jnp.dot(merged.astype(BF16), wo_ref[...], preferred_element_type=F32)


def _merge(x, ua, ub, uc, p, bg, wa, wb, wc, wo, *, tm=512):
    t, d = x.shape
    row = lambda i: (i, 0)
    act = pl.BlockSpec((tm, d), row)
    return pl.pallas_call(
        _merge_kernel,
        grid=(t // tm,),
        in_specs=[act, act, act, act, pl.BlockSpec((tm, 3 * d), lambda i: (i, P_GATES // (3 * d))),
                  _resident(bg.shape), _resident(wa.shape), _resident(wb.shape), _resident(wc.shape),
                  _resident(wo.shape)],
        out_specs=act,
        out_shape=jax.ShapeDtypeStruct((t, d), F32),
        compiler_params=_params("parallel"),
    )(x, ua, ub, uc, p, bg, wa, wb, wc, wo)


def _split_w_in(w):
    pts = [0]
    for s in IN_SIZES:
        pts.append(pts[-1] + s)
    gates, z, xbc, dt, q, k, v, pw = (w[:, pts[n]:pts[n + 1]] for n in range(len(IN_SIZES)))
    q = q * (DA_HEAD_DIM ** -0.5 * math.log2(math.e))
    w_main = jnp.concatenate([gates, z, pw, q, k, v, xbc], axis=1).astype(BF16)
    w_dt = jnp.pad(dt, ((0, 0), (0, LANES - SSD_HEADS))).astype(BF16)
    return w_main, w_dt


def _pad_lanes(v):
    return jnp.pad(v, (0, LANES - v.shape[0]))[None, :]


def kernel(x, ffn1_norm, ffn1_w13, ffn1_w2, mix_norm, w_in, b_gate, ssd_conv_w, ssd_conv_b, ssd_dt_bias,
           ssd_a_log, ssd_d, ssd_norm, ssd_wo, da_lq1, da_lk1, da_lq2, da_lk2, da_subln, da_wo, cv_dw_w,
           cv_dw_b, cv_ln_g, cv_ln_b, cv_wo, w_out, ffn2_norm, ffn2_w13, ffn2_w2, final_norm):
    batch, seq, d = x.shape
    depth = ffn1_norm.shape[0]
    xf = x.reshape(batch * seq, d)
    head_of_channel = jnp.arange(D_MODEL, dtype=jnp.int32) // SSD_HEAD_DIM
    expand = (jnp.arange(LANES, dtype=jnp.int32)[:, None] == head_of_channel[None, :]).astype(BF16)
    expand = jnp.concatenate([expand, expand], axis=0)
    for l in range(depth):
        lambda_init = 0.8 - 0.6 * math.exp(-0.3 * l)
        w_main, w_dt = _split_w_in(w_in[l])
        xf, h = _ffn(xf, ffn1_norm[l][None, :], ffn1_w13[l].astype(BF16), ffn1_w2[l].astype(BF16),
                     mode="mix", extra=(mix_norm[l][None, :],))
        p = _proj(h, w_main)
        ua = _ssd(p, h, w_dt, ssd_conv_w[l], ssd_conv_b[l][None, :], _pad_lanes(ssd_dt_bias[l]),
                  _pad_lanes(-jnp.exp(ssd_a_log[l]) * math.log2(math.e)),
                  jnp.repeat(ssd_d[l], SSD_HEAD_DIM)[None, :],
                  ssd_norm[l][None, :], expand, batch=batch, seq=seq)
        lam_rows = jnp.pad(jnp.stack([da_lq1[l], da_lk1[l], da_lq2[l], da_lk2[l]]),
                           ((0, SUBLANES - 4), (0, LANES - DA_HEAD_DIM)))
        ub = _attn(p, lam_rows, da_subln[l][None, :], batch=batch, seq=seq, lambda_init=lambda_init)
        uc = _conv(p, cv_dw_w[l], cv_dw_b[l][None, :], cv_ln_g[l][None, :], cv_ln_b[l][None, :],
                   batch=batch, seq=seq)
        xf = _merge(xf, ua, ub, uc, p, b_gate[l][None, :], ssd_wo[l].astype(BF16), da_wo[l].astype(BF16),
                    cv_wo[l].astype(BF16), w_out[l].astype(BF16))
        if l == depth - 1:
            xf = _ffn(xf, ffn2_norm[l][None, :], ffn2_w13[l].astype(BF16), ffn2_w2[l].astype(BF16),
                      mode="final", extra=(final_norm[None, :],))
        else:
            xf = _ffn(xf, ffn2_norm[l][None, :], ffn2_w13[l].astype(BF16), ffn2_w2[l].astype(BF16),
                      mode="plain")
    return xf.reshape(batch, seq, d)
```

```python
import functools
import math

import jax
import jax.numpy as jnp
from jax import lax
from jax.experimental import pallas as pl
from jax.experimental.pallas import tpu as pltpu

F32 = jnp.float32
BF16 = jnp.bfloat16

D_MODEL = 1024
D_FF = 2816
EPS = 1e-6
SSD_HEADS = 16
SSD_HEAD_DIM = 64
SSD_GROUPS = 2
SSD_STATE = 128
SSD_CONV = 4
SSD_CHUNK = 128
SSD_XBC = 1536
DA_HEADS = 8
DA_HEAD_DIM = 64
DA_EPS = 1e-5
LN_EPS = 1e-5
CONV_WIDTH = 31
IN_SIZES = (3 * D_MODEL, D_MODEL, SSD_XBC, SSD_HEADS, D_MODEL, D_MODEL, D_MODEL, 2 * D_MODEL)

LANES = 128
SUBLANES = 8
VMEM_LIMIT_BYTES = 56 * 1024 * 1024

P_GATES = 0
P_Z = 3072
P_PW = 4096
P_Q = 6144
P_K = 7168
P_V = 8192
P_XBC = 9216
P_COLS = 10752


def _params(*sem):
    return pltpu.CompilerParams(dimension_semantics=sem, vmem_limit_bytes=VMEM_LIMIT_BYTES)


def _resident(shape):
    nd = len(shape)
    return pl.BlockSpec(shape, lambda *_: (0,) * nd, pipeline_mode=pl.Buffered(1))


def _sigmoid(x):
    return 1.0 / (1.0 + jnp.exp(-x))


def _silu(x):
    return x * _sigmoid(x)


def _ffn_kernel(*refs, f_chunks, mode):
    if mode == "mix":
        x_ref, g_ref, w13_ref, w2_ref, gm_ref, xo_ref, h_ref = refs
    elif mode == "final":
        x_ref, g_ref, w13_ref, w2_ref, gf_ref, xo_ref = refs
    else:
        x_ref, g_ref, w13_ref, w2_ref, xo_ref = refs
    x = x_ref[...]
    h = (x * lax.rsqrt(jnp.mean(x * x, axis=-1, keepdims=True) + EPS) * g_ref[...]).astype(BF16)
    acc = jnp.zeros(x.shape, F32)
    lo = 0
    for width in f_chunks:
        gate = jnp.dot(h, w13_ref[:, lo:lo + width], preferred_element_type=F32)
        up = jnp.dot(h, w13_ref[:, D_FF + lo:D_FF + lo + width], preferred_element_type=F32)
        act = (_silu(gate) * up).astype(BF16)
        acc = acc + jnp.dot(act, w2_ref[lo:lo + width, :], preferred_element_type=F32)
        lo += width
    xn = x + 0.5 * acc
    if mode == "final":
        xo_ref[...] = xn * lax.rsqrt(jnp.mean(xn * xn, axis=-1, keepdims=True) + EPS) * gf_ref[...]
        return
    xo_ref[...] = xn
    if mode == "mix":
        hm = xn * lax.rsqrt(jnp.mean(xn * xn, axis=-1, keepdims=True) + EPS) * gm_ref[...]
        h_ref[...] = hm.astype(h_ref.dtype)


MXU_TILE = 256
FFN_CHUNKS = (6 * MXU_TILE, D_FF - 6 * MXU_TILE)


def _ffn(x, g, w13, w2, *, mode, extra=(), tm=512, f_chunks=FFN_CHUNKS):
    t, d = x.shape
    row = lambda i: (i, 0)
    in_specs = [pl.BlockSpec((tm, d), row), _resident((1, d)), _resident(w13.shape), _resident(w2.shape)]
    out_shape = [jax.ShapeDtypeStruct((t, d), F32)]
    out_specs = [pl.BlockSpec((tm, d), row)]
    if mode == "mix":
        in_specs += [_resident((1, d))]
        out_shape += [jax.ShapeDtypeStruct((t, d), BF16)]
        out_specs += [pl.BlockSpec((tm, d), row)]
    elif mode == "final":
        in_specs += [_resident((1, d))]
    out = pl.pallas_call(
        functools.partial(_ffn_kernel, f_chunks=f_chunks, mode=mode),
        grid=(t // tm,),
        in_specs=in_specs,
        out_specs=out_specs,
        out_shape=out_shape,
        compiler_params=_params("parallel"),
    )(x, g, w13, w2, *extra)
    return out if mode == "mix" else out[0]


def _proj_kernel(h_ref, w_ref, o_ref):
    o_ref[...] = jnp.dot(h_ref[...], w_ref[...], preferred_element_type=F32).astype(o_ref.dtype)


def _proj(h, w, *, tm=1024, tn=1792):
    t, d = h.shape
    n = w.shape[1]
    return pl.pallas_call(
        _proj_kernel,
        grid=(n // tn, t // tm),
        in_specs=[pl.BlockSpec((tm, d), lambda j, i: (i, 0)), pl.BlockSpec((d, tn), lambda j, i: (0, j))],
        out_specs=pl.BlockSpec((tm, tn), lambda j, i: (i, j)),
        out_shape=jax.ShapeDtypeStruct((t, n), BF16),
        compiler_params=_params("parallel", "parallel"),
    )(h, w)


def _ssd_kernel(xbc_ref, z_ref, h_ref, wdt_ref, cw_ref, cb_ref, dtb_ref, aneg_ref, dexp_ref, nrm_ref,
                expand_ref, o_ref, xwin_ref, state_ref):
    c = pl.program_id(1)
    L = SSD_CHUNK
    tail = SUBLANES
    rows_all = xbc_ref.shape[0]

    @pl.when(c == 0)
    def _():
        xwin_ref[0:tail, :] = jnp.zeros((tail, SSD_XBC), F32)
        state_ref[...] = jnp.zeros(state_ref.shape, F32)

    @pl.when(c != 0)
    def _():
        xwin_ref[0:tail, :] = xwin_ref[rows_all:rows_all + tail, :]

    xwin_ref[tail:tail + rows_all, :] = xbc_ref[...].astype(F32)
    win = xwin_ref[0:tail + rows_all, :]
    conv = cb_ref[...] + cw_ref[SSD_CONV - 1:SSD_CONV, :] * win[tail:tail + rows_all]
    for j in range(SSD_CONV - 1):
        shifted = pltpu.roll(win, SSD_CONV - 1 - j, axis=0)
        conv = conv + cw_ref[j:j + 1, :] * shifted[tail:tail + rows_all]
    xbc_all = _silu(conv)

    dt_in = jnp.dot(h_ref[...], wdt_ref[...], preferred_element_type=F32) + dtb_ref[...]
    dt_all = jnp.maximum(dt_in, 0.0) + jnp.log(1.0 + jnp.exp(-jnp.abs(dt_in)))
    a_all = dt_all * aneg_ref[...]
    row = lax.broadcasted_iota(jnp.int32, (L, L), 0)
    col = lax.broadcasted_iota(jnp.int32, (L, L), 1)
    causal = col <= row
    for r0 in range(0, rows_all, L):
        _ssd_chunk(xbc_all[r0:r0 + L], dt_all[r0:r0 + L], a_all[r0:r0 + L], causal,
                   z_ref.at[r0:r0 + L], o_ref.at[r0:r0 + L], dexp_ref, nrm_ref, expand_ref, state_ref)


def _ssd_chunk(xbc, dt, a, causal, z_ref, o_ref, dexp_ref, nrm_ref, expand_ref, state_ref):
    L = SSD_CHUNK
    xs = xbc[:, :D_MODEL]

    def split(v, terms):
        parts = []
        for _ in range(terms):
            piece = v.astype(BF16)
            parts.append(piece)
            v = v - piece.astype(F32)
        return parts

    tril = jnp.where(causal, 1.0, 0.0).astype(BF16)
    a_cum = jnp.dot(jnp.concatenate([tril] * 3, axis=1), jnp.concatenate(split(a, 3), axis=0),
                    preferred_element_type=F32)
    a_cum_t = a_cum.T
    a_last = a_cum[L - 1:L, :]
    exp_a = jnp.exp2(a_cum)
    dstate = jnp.exp2(a_last - a_cum)
    per_head = jnp.concatenate([jnp.concatenate(split(v, 2), axis=1) for v in (dt, dt * dstate, exp_a)], axis=0)
    per_chan = jnp.dot(per_head, expand_ref[...], preferred_element_type=F32)
    dt_x, dtd_x, expa_x = per_chan[0:L], per_chan[L:2 * L], per_chan[2 * L:3 * L]

    xdt = xs * dt_x
    xdt_state = (xs * dtd_x).astype(BF16)
    lane = lax.broadcasted_iota(jnp.int32, (L, D_MODEL), 1)
    lo_half = (lane & (LANES - 1)) < SSD_HEAD_DIM
    xdt_lo = jnp.where(lo_half, xdt, 0.0).astype(BF16)
    xdt_hi = jnp.where(lo_half, 0.0, xdt).astype(BF16)

    gw = SSD_STATE
    cg = D_MODEL // SSD_GROUPS
    hg = SSD_HEADS // SSD_GROUPS
    y_parts = []
    for g in range(SSD_GROUPS):
        bm = xbc[:, D_MODEL + g * gw:D_MODEL + (g + 1) * gw]
        cm = xbc[:, D_MODEL + SSD_GROUPS * gw + g * gw:D_MODEL + SSD_GROUPS * gw + (g + 1) * gw]
        bm16 = bm.astype(BF16)
        cm16 = cm.astype(BF16)
        cb = lax.dot_general(cm16, bm16, (((1,), (1,)), ((), ())), preferred_element_type=F32)
        for pair in range(hg // 2):
            ms = []
            for k in range(2):
                h = g * hg + pair * 2 + k
                seg = a_cum[:, h:h + 1] - a_cum_t[h:h + 1, :]
                ms.append((cb * jnp.exp2(jnp.where(causal, seg, -jnp.inf))).astype(BF16))
            m2 = jnp.concatenate(ms, axis=1)
            c0 = (g * hg + pair * 2) * SSD_HEAD_DIM
            rhs = jnp.concatenate([xdt_lo[:, c0:c0 + LANES], xdt_hi[:, c0:c0 + LANES]], axis=0)
            y_parts.append(jnp.dot(m2, rhs, preferred_element_type=F32))
    y = jnp.concatenate(y_parts, axis=1)

    outs = []
    for g in range(SSD_GROUPS):
        bm = xbc[:, D_MODEL + g * gw:D_MODEL + (g + 1) * gw]
        cm = xbc[:, D_MODEL + SSD_GROUPS * gw + g * gw:D_MODEL + SSD_GROUPS * gw + (g + 1) * gw]
        prev = state_ref[g]
        y_off = jnp.dot(cm.astype(BF16), prev.astype(BF16), preferred_element_type=F32)
        new = jnp.dot(bm.T.astype(BF16), xdt_state[:, g * cg:(g + 1) * cg], preferred_element_type=F32)
        state_ref[g] = prev * expa_x[L - 1:L, g * cg:(g + 1) * cg] + new
        yg = y[:, g * cg:(g + 1) * cg] + y_off * expa_x[:, g * cg:(g + 1) * cg]
        yg = yg + xs[:, g * cg:(g + 1) * cg] * dexp_ref[:, g * cg:(g + 1) * cg]
        u = yg * _silu(z_ref[:, g * cg:(g + 1) * cg].astype(F32))
        u = u * lax.rsqrt(jnp.mean(u * u, axis=-1, keepdims=True) + EPS)
        outs.append(u * nrm_ref[:, g * cg:(g + 1) * cg])
    o_ref[...] = jnp.concatenate(outs, axis=1).astype(o_ref.dtype)


def _ssd(p, h, wdt, cw, cb, dtb, aneg, dexp, nrm, expand, *, batch, seq, chunks_per_step=2):
    t = batch * seq
    L = SSD_CHUNK * chunks_per_step
    nc = seq // L
    rowmap = lambda b, c: (b * nc + c, 0)
    return pl.pallas_call(
        _ssd_kernel,
        grid=(batch, nc),
        in_specs=[
            pl.BlockSpec((L, SSD_XBC), lambda b, c: (b * nc + c, P_XBC // SSD_XBC)),
            pl.BlockSpec((L, D_MODEL), lambda b, c: (b * nc + c, P_Z // D_MODEL)),
            pl.BlockSpec((L, D_MODEL), rowmap), _resident(wdt.shape),
            _resident(cw.shape), _resident(cb.shape), _resident(dtb.shape), _resident(aneg.shape),
            _resident(dexp.shape), _resident(nrm.shape), _resident(expand.shape),
        ],
        out_specs=pl.BlockSpec((L, D_MODEL), rowmap),
        out_shape=jax.ShapeDtypeStruct((t, D_MODEL), BF16),
        scratch_shapes=[
            pltpu.VMEM((L + 2 * SUBLANES, SSD_XBC), F32),
            pltpu.VMEM((SSD_GROUPS, SSD_STATE, D_MODEL // SSD_GROUPS), F32),
        ],
        compiler_params=_params("parallel", "arbitrary"),
    )(p, p, h, wdt, cw, cb, dtb, aneg, dexp, nrm, expand)


def _attn_kernel(q_ref, k_ref, v_ref, lam_ref, sub_ref, o_ref, q2_ref, m_ref, acc_ref, s0_ref, s1_ref, *,
                 tq, tk, nq, lambda_init):
    per = tq // tk
    first = lax.broadcasted_iota(jnp.int32, (tk, LANES), 1) < DA_HEAD_DIM

    def stack(t, carry):
        q = q_ref[pl.ds(pl.multiple_of(t * tk, tk), tk), :]
        zero = jnp.zeros_like(q)
        dst = pl.multiple_of(2 * t * tk, tk)
        q2_ref[pl.ds(dst, tk), :] = jnp.where(first, q, zero)
        q2_ref[pl.ds(dst + tk, tk), :] = jnp.where(first, zero, q)
        return carry

    lax.fori_loop(0, nq * per, stack, 0)
    ones = jnp.ones((tk, LANES), BF16)
    nt = (((1,), (1,)), ((), ()))
    s_refs = (s0_ref, s1_ref)

    def scores(qi, j, dst_ref, lo):
        rows = pl.ds(pl.multiple_of(qi * 2 * tq + lo, tk), 2 * tq - lo)
        keys = pl.ds(pl.multiple_of(j * tk, tk), tk)
        dst_ref[lo:2 * tq, :] = lax.dot_general(q2_ref[rows, :], k_ref[keys, :], nt, preferred_element_type=F32)

    def consume(j, src_ref, diag):
        lo = 0 if diag is None else 2 * diag * tk
        start = pl.multiple_of(j * tk, tk)
        v2 = jnp.concatenate([v_ref[pl.ds(start, tk), :], ones], axis=1)
        s = src_ref[lo:2 * tq, :]
        if diag is not None:
            top = s[0:2 * tk]
            r = lax.broadcasted_iota(jnp.int32, top.shape, 0) & (tk - 1)
            cc = lax.broadcasted_iota(jnp.int32, top.shape, 1)
            top = jnp.where(cc <= r, top, -jnp.inf)
            s = top if diag == per - 1 else jnp.concatenate([top, s[2 * tk:]], axis=0)
        m_old = m_ref[lo:2 * tq, :]
        m_new = jnp.maximum(m_old, jnp.max(s, axis=-1, keepdims=True))
        alpha = jnp.exp2(m_old - m_new)
        p = jnp.exp2(s - jnp.concatenate([m_new] * (tk // LANES), axis=1))
        acc_ref[lo:2 * tq, :] = (jnp.concatenate([alpha, alpha], axis=1) * acc_ref[lo:2 * tq, :]
                                 + jnp.dot(p.astype(BF16), v2, preferred_element_type=F32))
        m_ref[lo:2 * tq, :] = m_new

    lp = lam_ref[...]
    lam = (jnp.exp(jnp.sum(lp[0:1] * lp[1:2], axis=-1, keepdims=True))
           - jnp.exp(jnp.sum(lp[2:3] * lp[3:4], axis=-1, keepdims=True)) + lambda_init)

    scores(0, 0, s_refs[0], 0)

    def query_tile(qi, carry):
        m_ref[...] = jnp.full(m_ref.shape, -jnp.inf, F32)
        acc_ref[...] = jnp.zeros(acc_ref.shape, F32)

        def tiles(first_tile, count):
            for d in range(count):
                j = first_tile + d
                scores(qi, j + 1, s_refs[(d + 1) % 2], 0)
                consume(j, s_refs[d % 2], None)

        def body(g, c):
            tiles(g * 2 * per, 2 * per)
            return c

        lax.fori_loop(0, qi // 2, body, 0)

        @pl.when(qi % 2 == 1)
        def _():
            tiles((qi - 1) * per, per)

        for d in range(per):
            j = qi * per + d
            if d + 1 < per:
                scores(qi, j + 1, s_refs[(d + 1) % 2], 2 * (d + 1) * tk)
            else:
                scores(jnp.minimum(qi + 1, nq - 1), 0, s_refs[0], 0)
            consume(j, s_refs[d % 2], d)

        for b in range(per):
            r0, r1 = 2 * b * tk, (2 * b + 1) * tk
            o = (acc_ref[r0:r0 + tk, 0:LANES] / acc_ref[r0:r0 + tk, LANES:2 * LANES]
                 - lam * (acc_ref[r1:r1 + tk, 0:LANES] / acc_ref[r1:r1 + tk, LANES:2 * LANES]))
            o = o * lax.rsqrt(jnp.mean(o * o, axis=-1, keepdims=True) + DA_EPS) * sub_ref[...]
            out_rows = pl.ds(pl.multiple_of(qi * tq + b * tk, tk), tk)
            o_ref[out_rows, :] = (o * (1.0 - lambda_init)).astype(o_ref.dtype)
        return carry

    lax.fori_loop(0, nq, query_tile, 0)


def _attn(p, lam_rows, subln, *, batch, seq, lambda_init, tq=1024, tk=512):
    t = batch * seq
    tq = min(tq, seq)
    assert tk & (tk - 1) == 0 and seq % tq == 0 and tq % (2 * tk) == 0
    nq = seq // tq
    head_block = lambda col: pl.BlockSpec((seq, LANES), lambda b, h: (b, col // LANES + h))
    return pl.pallas_call(
        functools.partial(_attn_kernel, tq=tq, tk=tk, nq=nq, lambda_init=lambda_init),
        grid=(batch, DA_HEADS),
        in_specs=[head_block(P_Q), head_block(P_K), head_block(P_V),
                  _resident(lam_rows.shape), _resident(subln.shape)],
        out_specs=pl.BlockSpec((seq, LANES), lambda b, h: (b, h)),
        out_shape=jax.ShapeDtypeStruct((t, D_MODEL), BF16),
        scratch_shapes=[
            pltpu.VMEM((2 * seq, LANES), BF16), pltpu.VMEM((2 * tq, LANES), F32),
            pltpu.VMEM((2 * tq, 2 * LANES), F32),
            pltpu.VMEM((2 * tq, tk), F32), pltpu.VMEM((2 * tq, tk), F32),
        ],
        compiler_params=_params("parallel", "parallel"),
    )(p, p, p, lam_rows, subln)


def _conv_kernel(pw_ref, w_ref, b_ref, o_ref, win_ref, *, tm, rows, lanes):
    i = pl.program_id(1)
    halo = 4 * SUBLANES

    @pl.when(i == 0)
    def _():
        win_ref[0:halo, :] = jnp.zeros((halo, D_MODEL), F32)

    @pl.when(i != 0)
    def _():
        win_ref[0:halo, :] = win_ref[tm:tm + halo, :]

    pa = pw_ref[:, :D_MODEL].astype(F32)
    pg = pw_ref[:, D_MODEL:].astype(F32)
    win_ref[halo:halo + tm, :] = pa * _sigmoid(pg)
    first = halo - (CONV_WIDTH - 1)
    for r0 in range(0, tm, rows):
        for c0 in range(0, D_MODEL, lanes):
            cols = slice(c0, c0 + lanes)
            acc = jnp.broadcast_to(b_ref[:, cols], (rows, lanes))
            for r in range(SUBLANES):
                n = rows if r == 0 else rows + SUBLANES
                z = None
                for o in range(r, halo + 1, SUBLANES):
                    if o < first:
                        continue
                    j = o - first
                    term = w_ref[j:j + 1, cols] * win_ref[r0 + o - r:r0 + o - r + n, cols]
                    z = term if z is None else z + term
                acc = acc + z[r:r + rows]
            o_ref[r0:r0 + rows, cols] = acc


def _conv(p, w, b, *, batch, seq, tm=256, rows=128, lanes=128):
    t = batch * seq
    nt = seq // tm
    return pl.pallas_call(
        functools.partial(_conv_kernel, tm=tm, rows=rows, lanes=lanes),
        grid=(batch, nt),
        in_specs=[
            pl.BlockSpec((tm, 2 * D_MODEL), lambda bb, i: (bb * nt + i, P_PW // (2 * D_MODEL))),
            _resident(w.shape), _resident(b.shape),
        ],
        out_specs=pl.BlockSpec((tm, D_MODEL), lambda bb, i: (bb * nt + i, 0)),
        out_shape=jax.ShapeDtypeStruct((t, D_MODEL), F32),
        scratch_shapes=[pltpu.VMEM((tm + 4 * SUBLANES, D_MODEL), F32)],
        compiler_params=_params("parallel", "arbitrary"),
    )(p, w, b)


def _merge_kernel(x_ref, ua_ref, ub_ref, yc_ref, gt_ref, bg_ref, lng_ref, lnb_ref, wa_ref, wb_ref, wc_ref, wo_ref,
                  o_ref):
    d = D_MODEL
    yc = yc_ref[...]
    mu = jnp.mean(yc, axis=-1, keepdims=True)
    dev = yc - mu
    var = jnp.mean(dev * dev, axis=-1, keepdims=True)
    uc = _silu(dev * lax.rsqrt(var + LN_EPS) * lng_ref[...] + lnb_ref[...]).astype(BF16)
    merged = None
    for n, (u, w_ref) in enumerate(((ua_ref[...], wa_ref), (ub_ref[...], wb_ref), (uc, wc_ref))):
        y = jnp.dot(u, w_ref[...], preferred_element_type=F32)
        gate = _sigmoid(gt_ref[:, n * d:(n + 1) * d].astype(F32) + bg_ref[:, n * d:(n + 1) * d])
        merged = gate * y if merged is None else merged + gate * y
    o_ref[...] = x_ref[...] + jnp.dot(merged.astype(BF16), wo_ref[...], preferred_element_type=F32)


def _merge(x, ua, ub, yc, p, bg, lng, lnb, wa, wb, wc, wo, *, tm=512):
    t, d = x.shape
    row = lambda i: (i, 0)
    act = pl.BlockSpec((tm, d), row)
    return pl.pallas_call(
        _merge_kernel,
        grid=(t // tm,),
        in_specs=[act, act, act, act, pl.BlockSpec((tm, 3 * d), lambda i: (i, P_GATES // (3 * d))),
                  _resident(bg.shape), _resident(lng.shape), _resident(lnb.shape),
                  _resident(wa.shape), _resident(wb.shape), _resident(wc.shape), _resident(wo.shape)],
        out_specs=act,
        out_shape=jax.ShapeDtypeStruct((t, d), F32),
        compiler_params=_params("parallel"),
    )(x, ua, ub, yc, p, bg, lng, lnb, wa, wb, wc, wo)


def _split_w_in(w):
    pts = [0]
    for s in IN_SIZES:
        pts.append(pts[-1] + s)
    gates, z, xbc, dt, q, k, v, pw = (w[:, pts[n]:pts[n + 1]] for n in range(len(IN_SIZES)))
    q = q * (DA_HEAD_DIM ** -0.5 * math.log2(math.e))
    w_main = jnp.concatenate([gates, z, pw, q, k, v, xbc], axis=1).astype(BF16)
    w_dt = jnp.pad(dt, ((0, 0), (0, LANES - SSD_HEADS))).astype(BF16)
    return w_main, w_dt


def _pad_lanes(v):
    return jnp.pad(v, (0, LANES - v.shape[0]))[None, :]


def kernel(x, ffn1_norm, ffn1_w13, ffn1_w2, mix_norm, w_in, b_gate, ssd_conv_w, ssd_conv_b, ssd_dt_bias,
           ssd_a_log, ssd_d, ssd_norm, ssd_wo, da_lq1, da_lk1, da_lq2, da_lk2, da_subln, da_wo, cv_dw_w,
           cv_dw_b, cv_ln_g, cv_ln_b, cv_wo, w_out, ffn2_norm, ffn2_w13, ffn2_w2, final_norm):
    batch, seq, d = x.shape
    depth = ffn1_norm.shape[0]
    xf = x.reshape(batch * seq, d)
    head_of_channel = jnp.arange(D_MODEL, dtype=jnp.int32) // SSD_HEAD_DIM
    expand = (jnp.arange(LANES, dtype=jnp.int32)[:, None] == head_of_channel[None, :]).astype(BF16)
    expand = jnp.concatenate([expand, expand], axis=0)
    for l in range(depth):
        lambda_init = 0.8 - 0.6 * math.exp(-0.3 * l)
        w_main, w_dt = _split_w_in(w_in[l])
        xf, h = _ffn(xf, ffn1_norm[l][None, :], ffn1_w13[l].astype(BF16), ffn1_w2[l].astype(BF16),
                     mode="mix", extra=(mix_norm[l][None, :],))
        p = _proj(h, w_main)
        ua = _ssd(p, h, w_dt, ssd_conv_w[l], ssd_conv_b[l][None, :], _pad_lanes(ssd_dt_bias[l]),
                  _pad_lanes(-jnp.exp(ssd_a_log[l]) * math.log2(math.e)),
                  jnp.repeat(ssd_d[l], SSD_HEAD_DIM)[None, :],
                  ssd_norm[l][None, :], expand, batch=batch, seq=seq)
        lam_rows = jnp.pad(jnp.stack([da_lq1[l], da_lk1[l], da_lq2[l], da_lk2[l]]),
                           ((0, SUBLANES - 4), (0, LANES - DA_HEAD_DIM)))
        ub = _attn(p, lam_rows, da_subln[l][None, :], batch=batch, seq=seq, lambda_init=lambda_init)
        yc = _conv(p, cv_dw_w[l], cv_dw_b[l][None, :], batch=batch, seq=seq)
        xf = _merge(xf, ua, ub, yc, p, b_gate[l][None, :], cv_ln_g[l][None, :], cv_ln_b[l][None, :],
                    ssd_wo[l].astype(BF16), da_wo[l].astype(BF16), cv_wo[l].astype(BF16), w_out[l].astype(BF16))
        if l == depth - 1:
            xf = _ffn(xf, ffn2_norm[l][None, :], ffn2_w13[l].astype(BF16), ffn2_w2[l].astype(BF16),
                      mode="final", extra=(final_norm[None, :],))
        else:
            xf = _ffn(xf, ffn2_norm[l][None, :], ffn2_w13[l].astype(BF16), ffn2_w2[l].astype(BF16),
                      mode="plain")
    return xf.reshape(batch, seq, d)
```

```python
import functools
import math

import jax
import jax.numpy as jnp
from jax import lax
from jax.experimental import pallas as pl
from jax.experimental.pallas import tpu as pltpu

F32 = jnp.float32
BF16 = jnp.bfloat16

D_MODEL = 1024
D_FF = 2816
EPS = 1e-6
SSD_HEADS = 16
SSD_HEAD_DIM = 64
SSD_GROUPS = 2
SSD_STATE = 128
SSD_CONV = 4
SSD_CHUNK = 128
SSD_XBC = 1536
DA_HEADS = 8
DA_HEAD_DIM = 64
DA_EPS = 1e-5
LN_EPS = 1e-5
CONV_WIDTH = 31
IN_SIZES = (3 * D_MODEL, D_MODEL, SSD_XBC, SSD_HEADS, D_MODEL, D_MODEL, D_MODEL, 2 * D_MODEL)

LANES = 128
SUBLANES = 8
VMEM_LIMIT_BYTES = 56 * 1024 * 1024

P_GATES = 0
P_Z = 3072
P_PW = 4096
P_Q = 6144
P_K = 7168
P_V = 8192
P_XBC = 9216
P_COLS = 10752


def _params(*sem):
    return pltpu.CompilerParams(dimension_semantics=sem, vmem_limit_bytes=VMEM_LIMIT_BYTES)


def _resident(shape):
    nd = len(shape)
    return pl.BlockSpec(shape, lambda *_: (0,) * nd, pipeline_mode=pl.Buffered(1))


def _sigmoid(x):
    return 1.0 / (1.0 + jnp.exp(-x))


def _silu(x):
    return x * _sigmoid(x)


def _ffn_kernel(*refs, f_chunks, mode):
    if mode == "mix":
        x_ref, g_ref, w13_ref, w2_ref, gm_ref, xo_ref, h_ref = refs
    elif mode == "final":
        x_ref, g_ref, w13_ref, w2_ref, gf_ref, xo_ref = refs
    else:
        x_ref, g_ref, w13_ref, w2_ref, xo_ref = refs
    x = x_ref[...]
    h = (x * lax.rsqrt(jnp.mean(x * x, axis=-1, keepdims=True) + EPS) * g_ref[...]).astype(BF16)
    acc = jnp.zeros(x.shape, F32)
    lo = 0
    for width in f_chunks:
        gate = jnp.dot(h, w13_ref[:, lo:lo + width], preferred_element_type=F32)
        up = jnp.dot(h, w13_ref[:, D_FF + lo:D_FF + lo + width], preferred_element_type=F32)
        act = (_silu(gate) * up).astype(BF16)
        acc = acc + jnp.dot(act, w2_ref[lo:lo + width, :], preferred_element_type=F32)
        lo += width
    xn = x + 0.5 * acc
    if mode == "final":
        xo_ref[...] = xn * lax.rsqrt(jnp.mean(xn * xn, axis=-1, keepdims=True) + EPS) * gf_ref[...]
        return
    xo_ref[...] = xn
    if mode == "mix":
        hm = xn * lax.rsqrt(jnp.mean(xn * xn, axis=-1, keepdims=True) + EPS) * gm_ref[...]
        h_ref[...] = hm.astype(h_ref.dtype)


MXU_TILE = 256
FFN_CHUNKS = (6 * MXU_TILE, D_FF - 6 * MXU_TILE)


def _ffn(x, g, w13, w2, *, mode, extra=(), tm=512, f_chunks=FFN_CHUNKS):
    t, d = x.shape
    row = lambda i: (i, 0)
    in_specs = [pl.BlockSpec((tm, d), row), _resident((1, d)), _resident(w13.shape), _resident(w2.shape)]
    out_shape = [jax.ShapeDtypeStruct((t, d), F32)]
    out_specs = [pl.BlockSpec((tm, d), row)]
    if mode == "mix":
        in_specs += [_resident((1, d))]
        out_shape += [jax.ShapeDtypeStruct((t, d), BF16)]
        out_specs += [pl.BlockSpec((tm, d), row)]
    elif mode == "final":
        in_specs += [_resident((1, d))]
    out = pl.pallas_call(
        functools.partial(_ffn_kernel, f_chunks=f_chunks, mode=mode),
        grid=(t // tm,),
        in_specs=in_specs,
        out_specs=out_specs,
        out_shape=out_shape,
        compiler_params=_params("parallel"),
    )(x, g, w13, w2, *extra)
    return out if mode == "mix" else out[0]


def _proj_kernel(h_ref, w_ref, o_ref):
    o_ref[...] = jnp.dot(h_ref[...], w_ref[...], preferred_element_type=F32).astype(o_ref.dtype)


def _proj(h, w, *, tm=2048, tn=1792):
    t, d = h.shape
    n = w.shape[1]
    return pl.pallas_call(
        _proj_kernel,
        grid=(n // tn, t // tm),
        in_specs=[pl.BlockSpec((tm, d), lambda j, i: (i, 0)), pl.BlockSpec((d, tn), lambda j, i: (0, j))],
        out_specs=pl.BlockSpec((tm, tn), lambda j, i: (i, j)),
        out_shape=jax.ShapeDtypeStruct((t, n), BF16),
        compiler_params=_params("parallel", "parallel"),
    )(h, w)


def _ssd_kernel(xbc_ref, z_ref, h_ref, wdt_ref, cw_ref, cb_ref, dtb_ref, aneg_ref, dexp_ref, nrm_ref,
                expand_ref, o_ref, xwin_ref, state_ref):
    c = pl.program_id(1)
    L = SSD_CHUNK
    tail = SUBLANES
    rows_all = xbc_ref.shape[0]

    @pl.when(c == 0)
    def _():
        xwin_ref[0:tail, :] = jnp.zeros((tail, SSD_XBC), F32)
        state_ref[...] = jnp.zeros(state_ref.shape, F32)

    @pl.when(c != 0)
    def _():
        xwin_ref[0:tail, :] = xwin_ref[rows_all:rows_all + tail, :]

    xwin_ref[tail:tail + rows_all, :] = xbc_ref[...].astype(F32)
    win = xwin_ref[0:tail + rows_all, :]
    conv = cb_ref[...] + cw_ref[SSD_CONV - 1:SSD_CONV, :] * win[tail:tail + rows_all]
    for j in range(SSD_CONV - 1):
        shifted = pltpu.roll(win, SSD_CONV - 1 - j, axis=0)
        conv = conv + cw_ref[j:j + 1, :] * shifted[tail:tail + rows_all]
    xbc_all = _silu(conv)

    dt_in = jnp.dot(h_ref[...], wdt_ref[...], preferred_element_type=F32) + dtb_ref[...]
    dt_all = jnp.maximum(dt_in, 0.0) + jnp.log(1.0 + jnp.exp(-jnp.abs(dt_in)))
    a_all = dt_all * aneg_ref[...]
    row = lax.broadcasted_iota(jnp.int32, (L, L), 0)
    col = lax.broadcasted_iota(jnp.int32, (L, L), 1)
    causal = col <= row
    for r0 in range(0, rows_all, L):
        _ssd_chunk(xbc_all[r0:r0 + L], dt_all[r0:r0 + L], a_all[r0:r0 + L], causal,
                   z_ref.at[r0:r0 + L], o_ref.at[r0:r0 + L], dexp_ref, nrm_ref, expand_ref, state_ref)


def _ssd_chunk(xbc, dt, a, causal, z_ref, o_ref, dexp_ref, nrm_ref, expand_ref, state_ref):
    L = SSD_CHUNK
    xs = xbc[:, :D_MODEL]

    def split(v, terms):
        parts = []
        for _ in range(terms):
            piece = v.astype(BF16)
            parts.append(piece)
            v = v - piece.astype(F32)
        return parts

    tril = jnp.where(causal, 1.0, 0.0).astype(BF16)
    a_cum = jnp.dot(jnp.concatenate([tril] * 3, axis=1), jnp.concatenate(split(a, 3), axis=0),
                    preferred_element_type=F32)
    a_cum_t = a_cum.T
    a_last = a_cum[L - 1:L, :]
    exp_a = jnp.exp2(a_cum)
    dstate = jnp.exp2(a_last - a_cum)
    per_head = jnp.concatenate([jnp.concatenate(split(v, 2), axis=1) for v in (dt, dt * dstate, exp_a)], axis=0)
    per_chan = jnp.dot(per_head, expand_ref[...], preferred_element_type=F32)
    dt_x, dtd_x, expa_x = per_chan[0:L], per_chan[L:2 * L], per_chan[2 * L:3 * L]

    xdt = xs * dt_x
    xdt_state = (xs * dtd_x).astype(BF16)
    lane = lax.broadcasted_iota(jnp.int32, (L, D_MODEL), 1)
    lo_half = (lane & (LANES - 1)) < SSD_HEAD_DIM
    xdt_lo = jnp.where(lo_half, xdt, 0.0).astype(BF16)
    xdt_hi = jnp.where(lo_half, 0.0, xdt).astype(BF16)

    gw = SSD_STATE
    cg = D_MODEL // SSD_GROUPS
    hg = SSD_HEADS // SSD_GROUPS
    y_parts = []
    for g in range(SSD_GROUPS):
        bm = xbc[:, D_MODEL + g * gw:D_MODEL + (g + 1) * gw]
        cm = xbc[:, D_MODEL + SSD_GROUPS * gw + g * gw:D_MODEL + SSD_GROUPS * gw + (g + 1) * gw]
        bm16 = bm.astype(BF16)
        cm16 = cm.astype(BF16)
        cb = lax.dot_general(cm16, bm16, (((1,), (1,)), ((), ())), preferred_element_type=F32)
        for pair in range(hg // 2):
            ms = []
            for k in range(2):
                h = g * hg + pair * 2 + k
                seg = a_cum[:, h:h + 1] - a_cum_t[h:h + 1, :]
                ms.append((cb * jnp.exp2(jnp.where(causal, seg, -jnp.inf))).astype(BF16))
            m2 = jnp.concatenate(ms, axis=1)
            c0 = (g * hg + pair * 2) * SSD_HEAD_DIM
            rhs = jnp.concatenate([xdt_lo[:, c0:c0 + LANES], xdt_hi[:, c0:c0 + LANES]], axis=0)
            y_parts.append(jnp.dot(m2, rhs, preferred_element_type=F32))
    y = jnp.concatenate(y_parts, axis=1)

    outs = []
    for g in range(SSD_GROUPS):
        bm = xbc[:, D_MODEL + g * gw:D_MODEL + (g + 1) * gw]
        cm = xbc[:, D_MODEL + SSD_GROUPS * gw + g * gw:D_MODEL + SSD_GROUPS * gw + (g + 1) * gw]
        prev = state_ref[g]
        y_off = jnp.dot(cm.astype(BF16), prev.astype(BF16), preferred_element_type=F32)
        new = jnp.dot(bm.T.astype(BF16), xdt_state[:, g * cg:(g + 1) * cg], preferred_element_type=F32)
        state_ref[g] = prev * expa_x[L - 1:L, g * cg:(g + 1) * cg] + new
        yg = y[:, g * cg:(g + 1) * cg] + y_off * expa_x[:, g * cg:(g + 1) * cg]
        yg = yg + xs[:, g * cg:(g + 1) * cg] * dexp_ref[:, g * cg:(g + 1) * cg]
        u = yg * _silu(z_ref[:, g * cg:(g + 1) * cg].astype(F32))
        u = u * lax.rsqrt(jnp.mean(u * u, axis=-1, keepdims=True) + EPS)
        outs.append(u * nrm_ref[:, g * cg:(g + 1) * cg])
    o_ref[...] = jnp.concatenate(outs, axis=1).astype(o_ref.dtype)


def _ssd(p, h, wdt, cw, cb, dtb, aneg, dexp, nrm, expand, *, batch, seq, chunks_per_step=2):
    t = batch * seq
    L = SSD_CHUNK * chunks_per_step
    nc = seq // L
    rowmap = lambda b, c: (b * nc + c, 0)
    return pl.pallas_call(
        _ssd_kernel,
        grid=(batch, nc),
        in_specs=[
            pl.BlockSpec((L, SSD_XBC), lambda b, c: (b * nc + c, P_XBC // SSD_XBC)),
            pl.BlockSpec((L, D_MODEL), lambda b, c: (b * nc + c, P_Z // D_MODEL)),
            pl.BlockSpec((L, D_MODEL), rowmap), _resident(wdt.shape),
            _resident(cw.shape), _resident(cb.shape), _resident(dtb.shape), _resident(aneg.shape),
            _resident(dexp.shape), _resident(nrm.shape), _resident(expand.shape),
        ],
        out_specs=pl.BlockSpec((L, D_MODEL), rowmap),
        out_shape=jax.ShapeDtypeStruct((t, D_MODEL), BF16),
        scratch_shapes=[
            pltpu.VMEM((L + 2 * SUBLANES, SSD_XBC), F32),
            pltpu.VMEM((SSD_GROUPS, SSD_STATE, D_MODEL // SSD_GROUPS), F32),
        ],
        compiler_params=_params("parallel", "arbitrary"),
    )(p, p, h, wdt, cw, cb, dtb, aneg, dexp, nrm, expand)


def _attn_kernel(q_ref, k_ref, v_ref, lam_ref, sub_ref, o_ref, q2_ref, m_ref, acc_ref, s0_ref, s1_ref, *,
                 tq, tk, nq, lambda_init):
    per = tq // tk
    first = lax.broadcasted_iota(jnp.int32, (tk, LANES), 1) < DA_HEAD_DIM

    def stack(t, carry):
        q = q_ref[pl.ds(pl.multiple_of(t * tk, tk), tk), :]
        zero = jnp.zeros_like(q)
        dst = pl.multiple_of(2 * t * tk, tk)
        q2_ref[pl.ds(dst, tk), :] = jnp.where(first, q, zero)
        q2_ref[pl.ds(dst + tk, tk), :] = jnp.where(first, zero, q)
        return carry

    lax.fori_loop(0, nq * per, stack, 0)
    ones = jnp.ones((tk, LANES), BF16)
    nt = (((1,), (1,)), ((), ()))
    s_refs = (s0_ref, s1_ref)

    def scores(qi, j, dst_ref, lo):
        rows = pl.ds(pl.multiple_of(qi * 2 * tq + lo, tk), 2 * tq - lo)
        keys = pl.ds(pl.multiple_of(j * tk, tk), tk)
        dst_ref[lo:2 * tq, :] = lax.dot_general(q2_ref[rows, :], k_ref[keys, :], nt, preferred_element_type=F32)

    def consume(j, src_ref, diag):
        lo = 0 if diag is None else 2 * diag * tk
        start = pl.multiple_of(j * tk, tk)
        v2 = jnp.concatenate([v_ref[pl.ds(start, tk), :], ones], axis=1)
        s = src_ref[lo:2 * tq, :]
        if diag is not None:
            top = s[0:2 * tk]
            r = lax.broadcasted_iota(jnp.int32, top.shape, 0) & (tk - 1)
            cc = lax.broadcasted_iota(jnp.int32, top.shape, 1)
            top = jnp.where(cc <= r, top, -jnp.inf)
            s = top if diag == per - 1 else jnp.concatenate([top, s[2 * tk:]], axis=0)
        m_old = m_ref[lo:2 * tq, :]
        m_new = jnp.maximum(m_old, jnp.max(s, axis=-1, keepdims=True))
        alpha = jnp.exp2(m_old - m_new)
        p = jnp.exp2(s - jnp.concatenate([m_new] * (tk // LANES), axis=1))
        acc_ref[lo:2 * tq, :] = (jnp.concatenate([alpha, alpha], axis=1) * acc_ref[lo:2 * tq, :]
                                 + jnp.dot(p.astype(BF16), v2, preferred_element_type=F32))
        m_ref[lo:2 * tq, :] = m_new

    lp = lam_ref[...]
    lam = (jnp.exp(jnp.sum(lp[0:1] * lp[1:2], axis=-1, keepdims=True))
           - jnp.exp(jnp.sum(lp[2:3] * lp[3:4], axis=-1, keepdims=True)) + lambda_init)

    scores(0, 0, s_refs[0], 0)

    def query_tile(qi, carry):
        m_ref[...] = jnp.full(m_ref.shape, -jnp.inf, F32)
        acc_ref[...] = jnp.zeros(acc_ref.shape, F32)

        def tiles(first_tile, count):
            for d in range(count):
                j = first_tile + d
                scores(qi, j + 1, s_refs[(d + 1) % 2], 0)
                consume(j, s_refs[d % 2], None)

        def body(g, c):
            tiles(g * 2 * per, 2 * per)
            return c

        lax.fori_loop(0, qi // 2, body, 0)

        @pl.when(qi % 2 == 1)
        def _():
            tiles((qi - 1) * per, per)

        for d in range(per):
            j = qi * per + d
            if d + 1 < per:
                scores(qi, j + 1, s_refs[(d + 1) % 2], 2 * (d + 1) * tk)
            else:
                scores(jnp.minimum(qi + 1, nq - 1), 0, s_refs[0], 0)
            consume(j, s_refs[d % 2], d)

        for b in range(per):
            r0, r1 = 2 * b * tk, (2 * b + 1) * tk
            o = (acc_ref[r0:r0 + tk, 0:LANES] / acc_ref[r0:r0 + tk, LANES:2 * LANES]
                 - lam * (acc_ref[r1:r1 + tk, 0:LANES] / acc_ref[r1:r1 + tk, LANES:2 * LANES]))
            o = o * lax.rsqrt(jnp.mean(o * o, axis=-1, keepdims=True) + DA_EPS) * sub_ref[...]
            out_rows = pl.ds(pl.multiple_of(qi * tq + b * tk, tk), tk)
            o_ref[out_rows, :] = (o * (1.0 - lambda_init)).astype(o_ref.dtype)
        return carry

    lax.fori_loop(0, nq, query_tile, 0)


def _attn(p, lam_rows, subln, *, batch, seq, lambda_init, tq=1024, tk=512):
    t = batch * seq
    tq = min(tq, seq)
    assert tk & (tk - 1) == 0 and seq % tq == 0 and tq % (2 * tk) == 0
    nq = seq // tq
    head_block = lambda col: pl.BlockSpec((seq, LANES), lambda b, h: (b, col // LANES + h))
    return pl.pallas_call(
        functools.partial(_attn_kernel, tq=tq, tk=tk, nq=nq, lambda_init=lambda_init),
        grid=(batch, DA_HEADS),
        in_specs=[head_block(P_Q), head_block(P_K), head_block(P_V),
                  _resident(lam_rows.shape), _resident(subln.shape)],
        out_specs=pl.BlockSpec((seq, LANES), lambda b, h: (b, h)),
        out_shape=jax.ShapeDtypeStruct((t, D_MODEL), BF16),
        scratch_shapes=[
            pltpu.VMEM((2 * seq, LANES), BF16), pltpu.VMEM((2 * tq, LANES), F32),
            pltpu.VMEM((2 * tq, 2 * LANES), F32),
            pltpu.VMEM((2 * tq, tk), F32), pltpu.VMEM((2 * tq, tk), F32),
        ],
        compiler_params=_params("parallel", "parallel"),
    )(p, p, p, lam_rows, subln)


def _conv_kernel(pw_ref, w_ref, b_ref, o_ref, win_ref, *, tm, rows, lanes):
    i = pl.program_id(1)
    halo = 4 * SUBLANES

    @pl.when(i == 0)
    def _():
        win_ref[0:halo, :] = jnp.zeros((halo, D_MODEL), F32)

    @pl.when(i != 0)
    def _():
        win_ref[0:halo, :] = win_ref[tm:tm + halo, :]

    pa = pw_ref[:, :D_MODEL].astype(F32)
    pg = pw_ref[:, D_MODEL:].astype(F32)
    win_ref[halo:halo + tm, :] = pa * _sigmoid(pg)
    first = halo - (CONV_WIDTH - 1)
    for r0 in range(0, tm, rows):
        for c0 in range(0, D_MODEL, lanes):
            cols = slice(c0, c0 + lanes)
            acc = jnp.broadcast_to(b_ref[:, cols], (rows, lanes))
            for r in range(SUBLANES):
                n = rows if r == 0 else rows + SUBLANES
                z = None
                for o in range(r, halo + 1, SUBLANES):
                    if o < first:
                        continue
                    j = o - first
                    term = w_ref[j:j + 1, cols] * win_ref[r0 + o - r:r0 + o - r + n, cols]
                    z = term if z is None else z + term
                acc = acc + z[r:r + rows]
            o_ref[r0:r0 + rows, cols] = acc


def _conv(p, w, b, *, batch, seq, tm=512, rows=256, lanes=128):
    t = batch * seq
    nt = seq // tm
    return pl.pallas_call(
        functools.partial(_conv_kernel, tm=tm, rows=rows, lanes=lanes),
        grid=(batch, nt),
        in_specs=[
            pl.BlockSpec((tm, 2 * D_MODEL), lambda bb, i: (bb * nt + i, P_PW // (2 * D_MODEL))),
            _resident(w.shape), _resident(b.shape),
        ],
        out_specs=pl.BlockSpec((tm, D_MODEL), lambda bb, i: (bb * nt + i, 0)),
        out_shape=jax.ShapeDtypeStruct((t, D_MODEL), F32),
        scratch_shapes=[pltpu.VMEM((tm + 4 * SUBLANES, D_MODEL), F32)],
        compiler_params=_params("parallel", "arbitrary"),
    )(p, w, b)


def _merge_kernel(x_ref, ua_ref, ub_ref, yc_ref, gt_ref, bg_ref, lng_ref, lnb_ref, wa_ref, wb_ref, wc_ref, wo_ref,
                  o_ref):
    d = D_MODEL
    yc = yc_ref[...]
    mu = jnp.mean(yc, axis=-1, keepdims=True)
    dev = yc - mu
    var = jnp.mean(dev * dev, axis=-1, keepdims=True)
    uc = _silu(dev * lax.rsqrt(var + LN_EPS) * lng_ref[...] + lnb_ref[...]).astype(BF16)
    merged = None
    for n, (u, w_ref) in enumerate(((ua_ref[...], wa_ref), (ub_ref[...], wb_ref), (uc, wc_ref))):
        y = jnp.dot(u, w_ref[...], preferred_element_type=F32)
        gate = _sigmoid(gt_ref[:, n * d:(n + 1) * d].astype(F32) + bg_ref[:, n * d:(n + 1) * d])
        merged = gate * y if merged is None else merged + gate * y
    o_ref[...] = x_ref[...] + jnp.dot(merged.astype(BF16), wo_ref[...], preferred_element_type=F32)


def _merge(x, ua, ub, yc, p, bg, lng, lnb, wa, wb, wc, wo, *, tm=512):
    t, d = x.shape
    row = lambda i: (i, 0)
    act = pl.BlockSpec((tm, d), row)
    return pl.pallas_call(
        _merge_kernel,
        grid=(t // tm,),
        in_specs=[act, act, act, act, pl.BlockSpec((tm, 3 * d), lambda i: (i, P_GATES // (3 * d))),
                  _resident(bg.shape), _resident(lng.shape), _resident(lnb.shape),
                  _resident(wa.shape), _resident(wb.shape), _resident(wc.shape), _resident(wo.shape)],
        out_specs=act,
        out_shape=jax.ShapeDtypeStruct((t, d), F32),
        compiler_params=_params("parallel"),
    )(x, ua, ub, yc, p, bg, lng, lnb, wa, wb, wc, wo)


def _split_w_in(w):
    pts = [0]
    for s in IN_SIZES:
        pts.append(pts[-1] + s)
    gates, z, xbc, dt, q, k, v, pw = (w[:, pts[n]:pts[n + 1]] for n in range(len(IN_SIZES)))
    q = q * (DA_HEAD_DIM ** -0.5 * math.log2(math.e))
    w_main = jnp.concatenate([gates, z, pw, q, k, v, xbc], axis=1).astype(BF16)
    w_dt = jnp.pad(dt, ((0, 0), (0, LANES - SSD_HEADS))).astype(BF16)
    return w_main, w_dt


def _pad_lanes(v):
    return jnp.pad(v, (0, LANES - v.shape[0]))[None, :]


def kernel(x, ffn1_norm, ffn1_w13, ffn1_w2, mix_norm, w_in, b_gate, ssd_conv_w, ssd_conv_b, ssd_dt_bias,
           ssd_a_log, ssd_d, ssd_norm, ssd_wo, da_lq1, da_lk1, da_lq2, da_lk2, da_subln, da_wo, cv_dw_w,
           cv_dw_b, cv_ln_g, cv_ln_b, cv_wo, w_out, ffn2_norm, ffn2_w13, ffn2_w2, final_norm):
    batch, seq, d = x.shape
    depth = ffn1_norm.shape[0]
    xf = x.reshape(batch * seq, d)
    head_of_channel = jnp.arange(D_MODEL, dtype=jnp.int32) // SSD_HEAD_DIM
    expand = (jnp.arange(LANES, dtype=jnp.int32)[:, None] == head_of_channel[None, :]).astype(BF16)
    expand = jnp.concatenate([expand, expand], axis=0)
    for l in range(depth):
        lambda_init = 0.8 - 0.6 * math.exp(-0.3 * l)
        w_main, w_dt = _split_w_in(w_in[l])
        xf, h = _ffn(xf, ffn1_norm[l][None, :], ffn1_w13[l].astype(BF16), ffn1_w2[l].astype(BF16),
                     mode="mix", extra=(mix_norm[l][None, :],))
        p = _proj(h, w_main)
        ua = _ssd(p, h, w_dt, ssd_conv_w[l], ssd_conv_b[l][None, :], _pad_lanes(ssd_dt_bias[l]),
                  _pad_lanes(-jnp.exp(ssd_a_log[l]) * math.log2(math.e)),
                  jnp.repeat(ssd_d[l], SSD_HEAD_DIM)[None, :],
                  ssd_norm[l][None, :], expand, batch=batch, seq=seq)
        lam_rows = jnp.pad(jnp.stack([da_lq1[l], da_lk1[l], da_lq2[l], da_lk2[l]]),
                           ((0, SUBLANES - 4), (0, LANES - DA_HEAD_DIM)))
        ub = _attn(p, lam_rows, da_subln[l][None, :], batch=batch, seq=seq, lambda_init=lambda_init)
        yc = _conv(p, cv_dw_w[l], cv_dw_b[l][None, :], batch=batch, seq=seq)
        xf = _merge(xf, ua, ub, yc, p, b_gate[l][None, :], cv_ln_g[l][None, :], cv_ln_b[l][None, :],
                    ssd_wo[l].astype(BF16), da_wo[l].astype(BF16), cv_wo[l].astype(BF16), w_out[l].astype(BF16))
        if l == depth - 1:
            xf = _ffn(xf, ffn2_norm[l][None, :], ffn2_w13[l].astype(BF16), ffn2_w2[l].astype(BF16),
                      mode="final", extra=(final_norm[None, :],))
        else:
            xf = _ffn(xf, ffn2_norm[l][None, :], ffn2_w13[l].astype(BF16), ffn2_w2[l].astype(BF16),
                      mode="plain")
    return xf.reshape(batch, seq, d)
```
